```python
import math
import jax, jax.numpy as jnp
from jax import lax
import numpy as np

D_MODEL = 2048
BATCH = 1
SEQ = 8192
DEPTH = 1
DEC_BATCH = 32
DEC_SEQ = 4
PAST_LEN = 16384
PAGE_SIZE = 128

D_ATTN = D_MODEL // 2
D_CONV = D_MODEL - D_ATTN
N_HEADS = 8
DV = D_ATTN // N_HEADS
DK = DV // 2
CONV_WIDTH = 31
N_BUCKETS = 32
MAX_DISTANCE = 128
Q_BLOCK = 128
EPS = 1e-6
D_IN = 4 * D_ATTN + 3 * D_CONV
SPLITS = [D_ATTN, 2 * D_ATTN, 3 * D_ATTN, 4 * D_ATTN,
          4 * D_ATTN + D_CONV, 4 * D_ATTN + 2 * D_CONV]

kernel_name = "hymba_diffattn_conformer_conv_step"


def rmsnorm(x, g):
    xf = x.astype(jnp.float32)
    y = xf * lax.rsqrt(jnp.mean(xf * xf, axis=-1, keepdims=True) + EPS)
    return (y * g.astype(jnp.float32)).astype(x.dtype)


def layernorm(x, g, b):
    xf = x.astype(jnp.float32)
    mu = jnp.mean(xf, axis=-1, keepdims=True)
    xc = xf - mu
    y = xc * lax.rsqrt(jnp.mean(xc * xc, axis=-1, keepdims=True) + EPS)
    return (y * g.astype(jnp.float32) + b.astype(jnp.float32)).astype(x.dtype)


def rel_bucket(dist):
    n = jnp.maximum(dist, 0)
    max_exact = N_BUCKETS // 2
    nf = jnp.maximum(n, 1).astype(jnp.float32)
    large = max_exact + (jnp.log(nf / max_exact) / math.log(MAX_DISTANCE / max_exact)
                         * (N_BUCKETS - max_exact)).astype(jnp.int32)
    large = jnp.minimum(large, N_BUCKETS - 1)
    return jnp.where(n < max_exact, n, large)


def diff_attn(q, k, v, q_pos, k_pos, rel_bias, lam):
    s = jnp.einsum('bqhpd,bkhpd->bhpqk', q, k).astype(jnp.float32) * (DK ** -0.5)
    bias = rel_bias.astype(jnp.float32)[rel_bucket(q_pos[:, None] - k_pos[None, :])]
    bias = jnp.transpose(bias, (2, 0, 1))[None, :, None]
    mask = (k_pos[None, :] <= q_pos[:, None])[None, None, None]
    s = jnp.where(mask, s + bias, -jnp.inf)
    p = jax.nn.softmax(s, axis=-1)
    a = p[:, :, 0] - lam * p[:, :, 1]
    return jnp.einsum('bhqk,bkhd->bqhd', a.astype(v.dtype), v)


def conv_module(c_a, c_b, buf, w_dw, b_dw, ln_g, ln_b, w_pw, b_pw):
    u = c_a * jax.nn.sigmoid(c_b)
    ext = jnp.concatenate([buf.astype(u.dtype), u], axis=1)
    y = lax.conv_general_dilated(ext, w_dw[:, None, :].astype(ext.dtype), (1,), 'VALID',
                                 dimension_numbers=('NWC', 'WIO', 'NWC'),
                                 feature_group_count=D_CONV) + b_dw
    y = jax.nn.silu(layernorm(y, ln_g, ln_b))
    y = y @ w_pw + b_pw
    return y, ext[:, -(CONV_WIDTH - 1):]


def mixer_layer(h, conv_buf, attend, w_in, lam_p, subln_g, w_dw, b_dw, ln_g, ln_b,
                w_pw, b_pw, w_out, lam_init):
    B, S, _ = h.shape
    z = h @ w_in
    q, k, v, g_a, c_a, c_b, g_c = jnp.split(z, SPLITS, axis=-1)
    q = q.reshape(B, S, N_HEADS, 2, DK)
    k = k.reshape(B, S, N_HEADS, 2, DK)
    v = v.reshape(B, S, N_HEADS, DV)
    lp = lam_p.astype(jnp.float32)
    lam = jnp.exp(jnp.dot(lp[0], lp[1])) - jnp.exp(jnp.dot(lp[2], lp[3])) + lam_init
    o = attend(q, k, v, lam)
    o = rmsnorm(o, subln_g) * (1.0 - lam_init)
    y_attn = o.reshape(B, S, D_ATTN) * jax.nn.silu(g_a)
    y_conv, new_buf = conv_module(c_a, c_b, conv_buf, w_dw, b_dw, ln_g, ln_b, w_pw, b_pw)
    y_conv = y_conv * jax.nn.silu(g_c)
    y = jnp.concatenate([y_attn, y_conv], axis=-1) @ w_out
    return y, k.reshape(B, S, N_HEADS, 2 * DK), v, new_buf


def setup_inputs(seed: int = 0) -> dict:
    key = jax.random.key(seed)
    ks = jax.random.split(key, 20)
    n_pages = PAST_LEN // PAGE_SIZE
    n_used = DEC_BATCH * n_pages
    n_pool = n_used + n_used // 4
    f32 = jnp.float32
    perm = jax.random.permutation(ks[0], n_pool)
    page_table = perm[:n_used].reshape(DEC_BATCH, n_pages).astype(jnp.int32)
    return {
        "x_prompt": jax.random.normal(ks[1], (BATCH, SEQ, D_MODEL), f32),
        "x_sample": jax.random.normal(ks[2], (DEC_BATCH, DEC_SEQ, D_MODEL), f32),
        "cache_k": jax.random.normal(ks[3], (DEPTH, n_pool, PAGE_SIZE, N_HEADS, 2 * DK), f32),
        "cache_v": jax.random.normal(ks[4], (DEPTH, n_pool, PAGE_SIZE, N_HEADS, DV), f32),
        "state_conv": jax.random.normal(ks[5], (DEPTH, DEC_BATCH, CONV_WIDTH - 1, D_CONV), f32),
        "page_table": page_table,
        "norm_g": 1.0 + 0.05 * jax.random.normal(ks[6], (DEPTH, D_MODEL), f32),
        "w_in": jax.random.normal(ks[7], (DEPTH, D_MODEL, D_IN), f32) * D_MODEL ** -0.5,
        "lam_p": 0.1 * jax.random.normal(ks[8], (DEPTH, 4, DK), f32),
        "subln_g": 1.0 + 0.05 * jax.random.normal(ks[9], (DEPTH, DV), f32),
        "w_dw": jax.random.normal(ks[10], (DEPTH, CONV_WIDTH, D_CONV), f32) * CONV_WIDTH ** -0.5,
        "b_dw": 0.01 * jax.random.normal(ks[11], (DEPTH, D_CONV), f32),
        "conv_ln_g": 1.0 + 0.05 * jax.random.normal(ks[12], (DEPTH, D_CONV), f32),
        "conv_ln_b": 0.01 * jax.random.normal(ks[13], (DEPTH, D_CONV), f32),
        "w_pw": jax.random.normal(ks[14], (DEPTH, D_CONV, D_CONV), f32) * D_CONV ** -0.5,
        "b_pw": 0.01 * jax.random.normal(ks[15], (DEPTH, D_CONV), f32),
        "w_out": jax.random.normal(ks[16], (DEPTH, D_MODEL, D_MODEL), f32) * D_MODEL ** -0.5,
        "rel_bias": 0.5 * jax.random.normal(ks[17], (N_BUCKETS, N_HEADS), f32),
        "final_g": 1.0 + 0.05 * jax.random.normal(ks[18], (D_MODEL,), f32),
    }


def reference(x_prompt, x_sample, cache_k, cache_v, state_conv, page_table,
              norm_g, w_in, lam_p, subln_g, w_dw, b_dw, conv_ln_g, conv_ln_b,
              w_pw, b_pw, w_out, rel_bias, final_g):
    xp, xs = x_prompt, x_sample
    B, S, _ = xp.shape
    DB, DS, _ = xs.shape
    nb = S // Q_BLOCK
    k_pos_prompt = jnp.arange(S, dtype=jnp.int32)
    q_pos_sample = PAST_LEN + jnp.arange(DS, dtype=jnp.int32)
    k_pos_sample = jnp.arange(PAST_LEN + DS, dtype=jnp.int32)

    nk_p, nv_p, nc_p, nk_s, nv_s, nc_s = [], [], [], [], [], []
    for l in range(DEPTH):
        lam_init = 0.8 - 0.6 * math.exp(-0.3 * l)
        kp_pool, vp_pool = cache_k[l], cache_v[l]

        def attend_prompt(q, k, v, lam):
            qb = jnp.moveaxis(q.reshape(B, nb, Q_BLOCK, N_HEADS, 2, DK), 1, 0)

            def one(args):
                qi, i = args
                q_pos = i * Q_BLOCK + jnp.arange(Q_BLOCK, dtype=jnp.int32)
                return diff_attn(qi, k, v, q_pos, k_pos_prompt, rel_bias, lam)

            o = lax.map(one, (qb, jnp.arange(nb, dtype=jnp.int32)))
            return jnp.moveaxis(o, 0, 1).reshape(B, S, N_HEADS, DV)

        def attend_sample(q, k, v, lam):
            def one(args):
                qb, kb, vb, ptb = args
                k_past = kp_pool[ptb].reshape(PAST_LEN, N_HEADS, 2, DK)
                v_past = vp_pool[ptb].reshape(PAST_LEN, N_HEADS, DV)
                k_all = jnp.concatenate([k_past.astype(kb.dtype), kb], axis=0)
                v_all = jnp.concatenate([v_past.astype(vb.dtype), vb], axis=0)
                return diff_attn(qb[None], k_all[None], v_all[None], q_pos_sample,
                                 k_pos_sample, rel_bias, lam)[0]

            return lax.map(one, (q, k, v, page_table))

        params = (w_in[l], lam_p[l], subln_g[l], w_dw[l], b_dw[l], conv_ln_g[l], conv_ln_b[l],
                  w_pw[l], b_pw[l], w_out[l], lam_init)
        buf0 = jnp.zeros((B, CONV_WIDTH - 1, D_CONV), xp.dtype)
        yp, kp, vp, cp = mixer_layer(rmsnorm(xp, norm_g[l]), buf0, attend_prompt, *params)
        ys, ks_, vs_, cs = mixer_layer(rmsnorm(xs, norm_g[l]), state_conv[l], attend_sample, *params)
        xp = xp + yp
        xs = xs + ys
        nk_p.append(kp); nv_p.append(vp); nc_p.append(cp)
        nk_s.append(ks_); nv_s.append(vs_); nc_s.append(cs)

    y_prompt = rmsnorm(xp, final_g)
    y_sample = rmsnorm(xs, final_g)
    new_k_prompt = jnp.stack(nk_p)
    new_v_prompt = jnp.stack(nv_p)
    new_conv_prompt = jnp.stack(nc_p)
    new_k_sample = jnp.stack(nk_s)
    new_v_sample = jnp.stack(nv_s)
    new_conv_sample = jnp.stack(nc_s)
    return (y_prompt, y_sample, new_k_prompt, new_v_prompt, new_conv_prompt,
            new_k_sample, new_v_sample, new_conv_sample)
```

```python
import functools
import math

import jax
import jax.numpy as jnp
import numpy as np
from jax import lax
from jax.experimental import pallas as pl
from jax.experimental.pallas import tpu as pltpu

F32 = jnp.float32
BF16 = jnp.bfloat16

N_HEADS = 8
DV = 128
DK = 64
D_ATTN = N_HEADS * DV
D_CONV = 1024
D_MODEL = D_ATTN + D_CONV
D_IN = 4 * D_ATTN + 3 * D_CONV
CONV_WIDTH = 31
HALO = CONV_WIDTH - 1
N_BUCKETS = 32
MAX_DISTANCE = 128
PAGE_SIZE = 128
EPS = 1e-6
NEG = -1e30

QB, KB, VB, GAB = 0, N_HEADS, 2 * N_HEADS, 3 * N_HEADS
CA_BLK, CB_BLK, GC_BLK = 4, 5, 6

VMEM_LIMIT = 56 * 1024 * 1024


def _params(sem, vmem=VMEM_LIMIT):
    return pltpu.CompilerParams(dimension_semantics=sem, vmem_limit_bytes=vmem)


def _bucket_thresholds():
    n = np.arange(0, MAX_DISTANCE + 1)
    max_exact = N_BUCKETS // 2
    nf = np.maximum(n, 1).astype(np.float32)
    large = max_exact + (np.log(nf / np.float32(max_exact)) / np.float32(math.log(MAX_DISTANCE / max_exact))
                         * np.float32(N_BUCKETS - max_exact)).astype(np.int32)
    large = np.minimum(large, N_BUCKETS - 1)
    bucket = np.where(n < max_exact, n, large)
    thr = []
    for b in range(1, N_BUCKETS):
        hit = np.nonzero(bucket >= b)[0]
        thr.append(int(hit[0]))
    assert bucket[-1] == N_BUCKETS - 1 and all(np.diff(bucket) >= 0)
    return tuple(thr)


_THR = _bucket_thresholds()


def _silu(x):
    return x * jax.nn.sigmoid(x)


def _inproj_kernel(x_ref, g_ref, w_ref, z_ref, h_ref):
    @pl.when(pl.program_id(1) == 0)
    def _():
        x = x_ref[...]
        ms = jnp.mean(x * x, axis=-1, keepdims=True)
        h_ref[...] = (x * lax.rsqrt(ms + EPS) * g_ref[...]).astype(BF16)

    z_ref[...] = jnp.dot(h_ref[...], w_ref[...], preferred_element_type=F32)


def _inproj(x, g, w_bf16, tm, tn):
    m = x.shape[0]
    return pl.pallas_call(
        _inproj_kernel,
        grid=(m // tm, D_IN // tn),
        in_specs=[
            pl.BlockSpec((tm, D_MODEL), lambda i, j: (i, 0)),
            pl.BlockSpec((1, D_MODEL), lambda i, j: (0, 0)),
            pl.BlockSpec((D_MODEL, tn), lambda i, j: (0, j)),
        ],
        out_specs=pl.BlockSpec((tm, tn), lambda i, j: (i, j)),
        out_shape=jax.ShapeDtypeStruct((m, D_IN), F32),
        scratch_shapes=[pltpu.VMEM((tm, D_MODEL), BF16)],
        compiler_params=_params(("arbitrary", "arbitrary")),
        name="inproj",
    )(x, g, w_bf16)


def _rel_bias_value(rb_ref, h, dist):
    val = jnp.full(dist.shape, rb_ref[0, h], F32)
    for b in range(1, N_BUCKETS):
        val = jnp.where(dist >= _THR[b - 1], rb_ref[b, h], val)
    val = val - rb_ref[N_BUCKETS - 1, h]
    return jnp.where(dist < 0, NEG, val)


def _bias_kernel(rb_ref, sub_ref, diag_ref, *, t):
    h = pl.program_id(0)
    r = lax.broadcasted_iota(jnp.int32, (t, t), 0)
    c = lax.broadcasted_iota(jnp.int32, (t, t), 1)
    diag_ref[0] = _rel_bias_value(rb_ref, h, r - c)
    sub_ref[0] = _rel_bias_value(rb_ref, h, r - c + t)


def _prompt_bias(rel_bias, t):
    shp = jax.ShapeDtypeStruct((N_HEADS, t, t), F32)
    return pl.pallas_call(
        functools.partial(_bias_kernel, t=t),
        grid=(N_HEADS,),
        in_specs=[pl.BlockSpec(memory_space=pltpu.SMEM)],
        out_specs=[pl.BlockSpec((1, t, t), lambda h: (h, 0, 0))] * 2,
        out_shape=[shp, shp],
        compiler_params=_params(("arbitrary",)),
        name="prompt_bias",
    )(rel_bias)


def _sample_bias_kernel(rb_ref, past_ref, new_ref, *, ds):
    h = pl.program_id(0)
    row = lax.broadcasted_iota(jnp.int32, (8, 128), 0)
    col = lax.broadcasted_iota(jnp.int32, (8, 128), 1)
    tq = row % ds
    past_ref[0] = _rel_bias_value(rb_ref, h, tq + 128 - col)
    new = _rel_bias_value(rb_ref, h, tq - col)
    new_ref[0] = jnp.where(col < ds, new, NEG)


def _sample_bias(rel_bias, ds):
    shp = jax.ShapeDtypeStruct((N_HEADS, 8, 128), F32)
    return pl.pallas_call(
        functools.partial(_sample_bias_kernel, ds=ds),
        grid=(N_HEADS,),
        in_specs=[pl.BlockSpec(memory_space=pltpu.SMEM)],
        out_specs=[pl.BlockSpec((1, 8, 128), lambda h: (h, 0, 0))] * 2,
        out_shape=[shp, shp],
        compiler_params=_params(("arbitrary",)),
        name="sample_bias",
    )(rel_bias)


def _lambda(lamp_ref, lam_init):
    lp = lamp_ref[...]
    a = jnp.sum(lp[0:1] * lp[1:2], axis=1, keepdims=True)
    b = jnp.sum(lp[2:3] * lp[3:4], axis=1, keepdims=True)
    return jnp.exp(a) - jnp.exp(b) + lam_init


def _head_epilogue(o0, o1, lam, sg, ga, lam_init):
    o = o0 - lam * o1
    ms = jnp.mean(o * o, axis=-1, keepdims=True)
    o = o * lax.rsqrt(ms + EPS) * sg * (1.0 - lam_init)
    return o * _silu(ga)


def _pattn_kernel(q_ref, k_ref, v_ref, ga_ref, bsub_ref, bdiag_ref, lamp_ref, sg_ref,
                  o_ref, kt_ref, vx_ref, m_ref, acc_ref, *, t, seq, lam_init):
    i = pl.program_id(1)
    prep = 512

    @pl.when(i == 0)
    def _prep():
        def body(c, carry):
            r0 = pl.multiple_of(c * prep, prep)
            kt_ref[:, pl.ds(r0, prep)] = k_ref[pl.ds(r0, prep), :].T.astype(BF16)
            vx_ref[pl.ds(r0, prep), 0:DV] = v_ref[pl.ds(r0, prep), :].astype(BF16)
            return carry
        lax.fori_loop(0, seq // prep, body, 0)
        vx_ref[:, DV:2 * DV] = jnp.ones((seq, DV), BF16)

    q = q_ref[...] * (DK ** -0.5)
    lane = lax.broadcasted_iota(jnp.int32, (t, 2 * DK), 1)
    qa = jnp.where(lane < DK, q, 0.0).astype(BF16)
    qb = jnp.where(lane >= DK, q, 0.0).astype(BF16)
    qs = jnp.concatenate([qa, qb], axis=0)

    m_ref[...] = jnp.full(m_ref.shape, NEG, F32)
    acc_ref[...] = jnp.zeros(acc_ref.shape, F32)

    def step(c0, bias):
        s = jnp.dot(qs, kt_ref[:, pl.ds(c0, t)], preferred_element_type=F32)
        if bias is not None:
            s = (s.reshape(2, t, t) + bias[None]).reshape(2 * t, t)
        m_prev = m_ref[...]
        m_next = jnp.maximum(m_prev, jnp.max(s, axis=1, keepdims=True))
        alpha = jnp.exp(m_prev - m_next)
        p = jnp.exp(s - jnp.tile(m_next, (1, t // 128)))
        pv = jnp.dot(p.astype(BF16), vx_ref[pl.ds(c0, t), :], preferred_element_type=F32)
        acc_ref[...] = jnp.tile(alpha, (1, 2)) * acc_ref[...] + pv
        m_ref[...] = m_next

    def full_body(j, carry):
        step(pl.multiple_of(j * t, t), None)
        return carry
    lax.fori_loop(0, i - 1, full_body, 0)

    @pl.when(i >= 1)
    def _sub():
        step(pl.multiple_of((i - 1) * t, t), bsub_ref[0])

    step(pl.multiple_of(i * t, t), bdiag_ref[0])

    acc = acc_ref[...]
    o_all = acc[:, 0:DV] / acc[:, DV:2 * DV]
    lam = _lambda(lamp_ref, lam_init)
    o_ref[...] = _head_epilogue(o_all[0:t], o_all[t:2 * t], lam, sg_ref[...], ga_ref[...], lam_init)


def _prompt_attention(z, bsub, bdiag, lam_p, subln_g, t, lam_init):
    seq = z.shape[0]
    kern = functools.partial(_pattn_kernel, t=t, seq=seq, lam_init=lam_init)
    return pl.pallas_call(
        kern,
        grid=(N_HEADS, seq // t),
        in_specs=[
            pl.BlockSpec((t, DV), lambda h, i: (i, QB + h)),
            pl.BlockSpec((seq, DV), lambda h, i: (0, KB + h)),
            pl.BlockSpec((seq, DV), lambda h, i: (0, VB + h)),
            pl.BlockSpec((t, DV), lambda h, i: (i, GAB + h)),
            pl.BlockSpec((1, t, t), lambda h, i: (h, 0, 0)),
            pl.BlockSpec((1, t, t), lambda h, i: (h, 0, 0)),
            pl.BlockSpec((4, DK), lambda h, i: (0, 0)),
            pl.BlockSpec((1, DV), lambda h, i: (0, 0)),
        ],
        out_specs=pl.BlockSpec((t, DV), lambda h, i: (i, h)),
        out_shape=jax.ShapeDtypeStruct((seq, D_ATTN), F32),
        scratch_shapes=[
            pltpu.VMEM((DV, seq), BF16),
            pltpu.VMEM((seq, 2 * DV), BF16),
            pltpu.VMEM((2 * t, 128), F32),
            pltpu.VMEM((2 * t, 2 * DV), F32),
        ],
        compiler_params=_params(("arbitrary", "arbitrary")),
        name="prompt_attention",
    )(z, z, z, z, bsub, bdiag, lam_p, subln_g)


def _sattn_kernel(pt_ref, zq_ref, ck_ref, cv_ref, bpast_ref, bnew_ref, lamp_ref, sg_ref,
                  o_ref, kbuf, vbuf, sem, *, ds, n_pages, ppc, lam_init):
    b = pl.program_id(0)
    nb = pl.num_programs(0)
    n_chunks = n_pages // ppc
    rows = PAGE_SIZE * N_HEADS
    npos = ppc * PAGE_SIZE

    def page_copies(req, chunk, slot):
        cps = []
        for pg in range(ppc):
            page = pt_ref[req, chunk * ppc + pg]
            cps.append(pltpu.make_async_copy(
                ck_ref.at[page], kbuf.at[slot, pl.ds(pg * rows, rows), :], sem.at[0, slot]))
            cps.append(pltpu.make_async_copy(
                cv_ref.at[page], vbuf.at[slot, pl.ds(pg * rows, rows), :], sem.at[1, slot]))
        return cps

    def start_chunk(req, chunk, slot):
        for cp in page_copies(req, chunk, slot):
            cp.start()

    def wait_chunk(req, chunk, slot):
        for cp in page_copies(req, chunk, slot):
            cp.wait()

    @pl.when(b == 0)
    def _prime():
        start_chunk(0, 0, 0)

    zq = zq_ref[0]
    lam = _lambda(lamp_ref, lam_init)
    row = lax.broadcasted_iota(jnp.int32, (2 * ds, 2 * DK), 0)
    lane = lax.broadcasted_iota(jnp.int32, (2 * ds, 2 * DK), 1)
    own_map = (lane >= DK) == (row >= ds)
    pad_rows = lax.broadcasted_iota(jnp.int32, (128, DV), 0) < ds

    qh = []
    for h in range(N_HEADS):
        qf = zq[:, (QB + h) * 128:(QB + h + 1) * 128] * (DK ** -0.5)
        q2 = jnp.concatenate([qf, qf], axis=0)
        qh.append(jnp.where(own_map, q2, 0.0).astype(BF16))

    def softmax_step(h, carry, s, vh):
        m_prev, l_prev, acc = carry[h]
        m_next = jnp.maximum(m_prev, jnp.max(s, axis=1, keepdims=True))
        alpha = jnp.exp(m_prev - m_next)
        p = jnp.exp(s - m_next)
        l_next = alpha * l_prev + jnp.sum(p, axis=1, keepdims=True)
        acc = alpha * acc + jnp.dot(p.astype(BF16), vh, preferred_element_type=F32)
        return (m_next, l_next, acc)

    def chunk_body(c, carry):
        slot = (b * n_chunks + c) % 2
        nxt = c + 1

        @pl.when(nxt < n_chunks)
        def _():
            start_chunk(b, nxt, 1 - slot)

        @pl.when(jnp.logical_and(nxt == n_chunks, b + 1 < nb))
        def _():
            start_chunk(b + 1, 0, 1 - slot)

        wait_chunk(b, c, slot)
        out = []
        for h in range(N_HEADS):
            kh = kbuf[slot, pl.ds(h, npos, stride=N_HEADS), :].astype(BF16)
            vh = vbuf[slot, pl.ds(h, npos, stride=N_HEADS), :].astype(BF16)
            s = lax.dot_general(qh[h], kh, (((1,), (1,)), ((), ())), preferred_element_type=F32)
            tail = jnp.where(c == n_chunks - 1, bpast_ref[h], 0.0)
            s = jnp.concatenate([s[:, :npos - 128], s[:, npos - 128:] + tail], axis=1)
            out.append(softmax_step(h, carry, s, vh))
        return tuple(out)

    init = tuple((jnp.full((2 * ds, 1), NEG, F32), jnp.zeros((2 * ds, 1), F32),
                  jnp.zeros((2 * ds, DV), F32)) for _ in range(N_HEADS))
    carry = lax.fori_loop(0, n_chunks, chunk_body, init)

    for h in range(N_HEADS):
        kn = zq[:, (KB + h) * 128:(KB + h + 1) * 128]
        vn = zq[:, (VB + h) * 128:(VB + h + 1) * 128]
        kn = jnp.where(pad_rows, jnp.concatenate([kn, jnp.zeros((128 - ds, DV), F32)], axis=0), 0.0)
        vn = jnp.where(pad_rows, jnp.concatenate([vn, jnp.zeros((128 - ds, DV), F32)], axis=0), 0.0)
        s = lax.dot_general(qh[h], kn.astype(BF16), (((1,), (1,)), ((), ())),
                            preferred_element_type=F32) + bnew_ref[h]
        _, l, acc = softmax_step(h, carry, s, vn.astype(BF16))
        o_all = acc / l
        ga = zq[:, (GAB + h) * 128:(GAB + h + 1) * 128]
        o_ref[0, :, h * DV:(h + 1) * DV] = _head_epilogue(
            o_all[0:ds], o_all[ds:2 * ds], lam, sg_ref[...], ga, lam_init)


def _sample_attention(zs, cache_k, cache_v, page_table, bpast, bnew, lam_p, subln_g, ds, lam_init,
                      ppc=8):
    db, n_pages = page_table.shape
    rows = PAGE_SIZE * N_HEADS
    n_pool = cache_k.shape[0]
    ck = cache_k.reshape(n_pool, rows, 2 * DK)
    cv = cache_v.reshape(n_pool, rows, DV)
    kern = functools.partial(_sattn_kernel, ds=ds, n_pages=n_pages, ppc=ppc, lam_init=lam_init)
    grid_spec = pltpu.PrefetchScalarGridSpec(
        num_scalar_prefetch=1,
        grid=(db,),
        in_specs=[
            pl.BlockSpec((1, ds, D_IN), lambda b, pt: (b, 0, 0)),
            pl.BlockSpec(memory_space=pl.ANY),
            pl.BlockSpec(memory_space=pl.ANY),
            pl.BlockSpec((N_HEADS, 8, 128), lambda b, pt: (0, 0, 0)),
            pl.BlockSpec((N_HEADS, 8, 128), lambda b, pt: (0, 0, 0)),
            pl.BlockSpec((4, DK), lambda b, pt: (0, 0)),
            pl.BlockSpec((1, DV), lambda b, pt: (0, 0)),
        ],
        out_specs=pl.BlockSpec((1, ds, D_ATTN), lambda b, pt: (b, 0, 0)),
        scratch_shapes=[
            pltpu.VMEM((2, ppc * rows, 2 * DK), F32),
            pltpu.VMEM((2, ppc * rows, DV), F32),
            pltpu.SemaphoreType.DMA((2, 2)),
        ],
    )
    return pl.pallas_call(
        kern,
        grid_spec=grid_spec,
        out_shape=jax.ShapeDtypeStruct((db, ds, D_ATTN), F32),
        compiler_params=_params(("arbitrary",)),
        name="sample_attention",
    )(page_table, zs.reshape(db, ds, D_IN), ck, cv, bpast, bnew, lam_p, subln_g)


def _ln_swish(y, lg, lb):
    mu = jnp.mean(y, axis=-1, keepdims=True)
    yc = y - mu
    var = jnp.mean(yc * yc, axis=-1, keepdims=True)
    return _silu(yc * lax.rsqrt(var + EPS) * lg + lb)


def _pconv_kernel(ca_ref, cb_ref, wdw_ref, bdw_ref, lg_ref, lb_ref, t_ref, nb_ref, ext_ref, *, ts, rc):
    i = pl.program_id(0)
    pad = 32

    @pl.when(i == 0)
    def _():
        ext_ref[0:pad, :] = jnp.zeros((pad, D_CONV), F32)

    ext_ref[pad:pad + ts, :] = ca_ref[...] * jax.nn.sigmoid(cb_ref[...])
    for r0 in range(0, ts, rc):
        acc = jnp.zeros((rc, D_CONV), F32)
        for j in range(CONV_WIDTH):
            lo = r0 + pad - HALO + j
            acc = acc + wdw_ref[j:j + 1, :] * ext_ref[lo:lo + rc, :]
        y = acc + bdw_ref[...]
        t_ref[r0:r0 + rc, :] = _ln_swish(y, lg_ref[...], lb_ref[...]).astype(BF16)
    nb_ref[...] = ext_ref[pad + ts - HALO:pad + ts, :]
    ext_ref[0:pad, :] = ext_ref[ts:ts + pad, :]


def _prompt_conv(z, w_dw, b_dw, ln_g, ln_b, ts=512, rc=32):
    seq = z.shape[0]
    vec = pl.BlockSpec((1, D_CONV), lambda i: (0, 0))
    return pl.pallas_call(
        functools.partial(_pconv_kernel, ts=ts, rc=rc),
        grid=(seq // ts,),
        in_specs=[
            pl.BlockSpec((ts, D_CONV), lambda i: (i, CA_BLK)),
            pl.BlockSpec((ts, D_CONV), lambda i: (i, CB_BLK)),
            pl.BlockSpec((CONV_WIDTH, D_CONV), lambda i: (0, 0)),
            vec, vec, vec,
        ],
        out_specs=[
            pl.BlockSpec((ts, D_CONV), lambda i: (i, 0)),
            pl.BlockSpec((HALO, D_CONV), lambda i: (0, 0)),
        ],
        out_shape=[
            jax.ShapeDtypeStruct((seq, D_CONV), BF16),
            jax.ShapeDtypeStruct((HALO, D_CONV), F32),
        ],
        scratch_shapes=[pltpu.VMEM((ts + 32, D_CONV), F32)],
        compiler_params=_params(("arbitrary",)),
        name="prompt_conv",
    )(z, z, w_dw, b_dw, ln_g, ln_b)


def _sconv_kernel(z_ref, st_ref, wdw_ref, bdw_ref, lg_ref, lb_ref, t_ref, nb_ref, ext_ref, *, ds):
    ca = z_ref[0, :, CA_BLK * D_CONV:(CA_BLK + 1) * D_CONV]
    cb = z_ref[0, :, CB_BLK * D_CONV:(CB_BLK + 1) * D_CONV]
    ext_ref[0:HALO, :] = st_ref[0]
    ext_ref[HALO:HALO + ds, :] = ca * jax.nn.sigmoid(cb)
    acc = jnp.zeros((ds, D_CONV), F32)
    for j in range(CONV_WIDTH):
        acc = acc + wdw_ref[j:j + 1, :] * ext_ref[j:j + ds, :]
    y = acc + bdw_ref[...]
    t_ref[0] = _ln_swish(y, lg_ref[...], lb_ref[...])
    nb_ref[0] = ext_ref[ds:ds + HALO, :]


def _sample_conv(zs3, state, w_dw, b_dw, ln_g, ln_b):
    db, ds, _ = zs3.shape
    vec = pl.BlockSpec((1, D_CONV), lambda b: (0, 0))
    return pl.pallas_call(
        functools.partial(_sconv_kernel, ds=ds),
        grid=(db,),
        in_specs=[
            pl.BlockSpec((1, ds, D_IN), lambda b: (b, 0, 0)),
            pl.BlockSpec((1, HALO, D_CONV), lambda b: (b, 0, 0)),
            pl.BlockSpec((CONV_WIDTH, D_CONV), lambda b: (0, 0)),
            vec, vec, vec,
        ],
        out_specs=[
            pl.BlockSpec((1, ds, D_CONV), lambda b: (b, 0, 0)),
            pl.BlockSpec((1, HALO, D_CONV), lambda b: (b, 0, 0)),
        ],
        out_shape=[
            jax.ShapeDtypeStruct((db, ds, D_CONV), F32),
            jax.ShapeDtypeStruct((db, HALO, D_CONV), F32),
        ],
        scratch_shapes=[pltpu.VMEM((HALO + 8, D_CONV), F32)],
        compiler_params=_params(("arbitrary",)),
        name="sample_conv",
    )(zs3, state, w_dw, b_dw, ln_g, ln_b)


def _outproj_kernel(x_ref, ya_ref, t_ref, gc_ref, wpw_ref, bpw_ref, wout_ref, fg_ref, y_ref):
    yc = jnp.dot(t_ref[...].astype(BF16), wpw_ref[...], preferred_element_type=F32) + bpw_ref[...]
    yc = yc * _silu(gc_ref[...])
    y = jnp.dot(ya_ref[...].astype(BF16), wout_ref[0:D_ATTN, :], preferred_element_type=F32)
    y = y + jnp.dot(yc.astype(BF16), wout_ref[D_ATTN:D_MODEL, :], preferred_element_type=F32)
    r = x_ref[...] + y
    ms = jnp.mean(r * r, axis=-1, keepdims=True)
    y_ref[...] = r * lax.rsqrt(ms + EPS) * fg_ref[...]


def _outproj(x, y_attn, t, z, w_pw_bf16, b_pw, w_out_bf16, final_g, tm):
    m = x.shape[0]
    return pl.pallas_call(
        _outproj_kernel,
        grid=(m // tm,),
        in_specs=[
            pl.BlockSpec((tm, D_MODEL), lambda i: (i, 0)),
            pl.BlockSpec((tm, D_ATTN), lambda i: (i, 0)),
            pl.BlockSpec((tm, D_CONV), lambda i: (i, 0)),
            pl.BlockSpec((tm, D_CONV), lambda i: (i, GC_BLK)),
            pl.BlockSpec((D_CONV, D_CONV), lambda i: (0, 0)),
            pl.BlockSpec((1, D_CONV), lambda i: (0, 0)),
            pl.BlockSpec((D_MODEL, D_MODEL), lambda i: (0, 0)),
            pl.BlockSpec((1, D_MODEL), lambda i: (0, 0)),
        ],
        out_specs=pl.BlockSpec((tm, D_MODEL), lambda i: (i, 0)),
        out_shape=jax.ShapeDtypeStruct((m, D_MODEL), F32),
        compiler_params=_params(("arbitrary",)),
        name="outproj",
    )(x, y_attn, t, z, w_pw_bf16, b_pw, w_out_bf16, final_g)


def kernel(x_prompt, x_sample, cache_k, cache_v, state_conv, page_table, norm_g, w_in, lam_p, subln_g,
           w_dw, b_dw, conv_ln_g, conv_ln_b, w_pw, b_pw, w_out, rel_bias, final_g):
    depth = w_in.shape[0]
    assert depth == 1, "single-layer step"
    bsz, seq, _ = x_prompt.shape
    assert bsz == 1
    db, ds, _ = x_sample.shape
    lam_init = 0.8 - 0.6 * math.exp(-0.3 * 0)
    t_attn = 512

    w_in_b = w_in[0].astype(BF16)
    w_pw_b = w_pw[0].astype(BF16)
    w_out_b = w_out[0].astype(BF16)
    g = norm_g[0][None, :]
    sg = subln_g[0][None, :]
    bdw, lg, lb, bpw = b_dw[0][None, :], conv_ln_g[0][None, :], conv_ln_b[0][None, :], b_pw[0][None, :]
    fg = final_g[None, :]

    xp = x_prompt.reshape(seq, D_MODEL)
    xs = x_sample.reshape(db * ds, D_MODEL)

    zp = _inproj(xp, g, w_in_b, tm=1024, tn=512)
    bsub, bdiag = _prompt_bias(rel_bias, t_attn)
    ya_p = _prompt_attention(zp, bsub, bdiag, lam_p[0], sg, t_attn, lam_init)
    t_p, nc_p = _prompt_conv(zp, w_dw[0], bdw, lg, lb)
    y_p = _outproj(xp, ya_p, t_p, zp, w_pw_b, bpw, w_out_b, fg, tm=256)

    zs = _inproj(xs, g, w_in_b, tm=db * ds, tn=512)
    bpast, bnew = _sample_bias(rel_bias, ds)
    ya_s = _sample_attention(zs, cache_k[0], cache_v[0], page_table, bpast, bnew, lam_p[0], sg, ds, lam_init)
    zs3 = zs.reshape(db, ds, D_IN)
    t_s, nc_s = _sample_conv(zs3, state_conv[0], w_dw[0], bdw, lg, lb)
    y_s = _outproj(xs, ya_s.reshape(db * ds, D_ATTN), t_s.reshape(db * ds, D_CONV), zs,
                   w_pw_b, bpw, w_out_b, fg, tm=db * ds)

    return (
        y_p.reshape(1, seq, D_MODEL),
        y_s.reshape(db, ds, D_MODEL),
        zp[:, D_ATTN:2 * D_ATTN].reshape(1, 1, seq, N_HEADS, 2 * DK),
        zp[:, 2 * D_ATTN:3 * D_ATTN].reshape(1, 1, seq, N_HEADS, DV),
        nc_p.reshape(1, 1, HALO, D_CONV),
        zs[:, D_ATTN:2 * D_ATTN].reshape(1, db, ds, N_HEADS, 2 * DK),
        zs[:, 2 * D_ATTN:3 * D_ATTN].reshape(1, db, ds, N_HEADS, DV),
        nc_s.reshape(1, db, HALO, D_CONV),
    )
```

```python
import functools
import math

import jax
import jax.numpy as jnp
import numpy as np
from jax import lax
from jax.experimental import pallas as pl
from jax.experimental.pallas import tpu as pltpu

F32 = jnp.float32
BF16 = jnp.bfloat16

N_HEADS = 8
DV = 128
DK = 64
D_ATTN = N_HEADS * DV
D_CONV = 1024
D_MODEL = D_ATTN + D_CONV
D_IN = 4 * D_ATTN + 3 * D_CONV
CONV_WIDTH = 31
HALO = CONV_WIDTH - 1
N_BUCKETS = 32
MAX_DISTANCE = 128
PAGE_SIZE = 128
EPS = 1e-6
NEG = -1e30

QB, KB, VB, GAB = 0, N_HEADS, 2 * N_HEADS, 3 * N_HEADS
CA_BLK, CB_BLK, GC_BLK = 4, 5, 6

VMEM_LIMIT = 56 * 1024 * 1024


def _params(sem, vmem=VMEM_LIMIT):
    return pltpu.CompilerParams(dimension_semantics=sem, vmem_limit_bytes=vmem)


def _bucket_thresholds():
    n = np.arange(0, MAX_DISTANCE + 1)
    max_exact = N_BUCKETS // 2
    nf = np.maximum(n, 1).astype(np.float32)
    large = max_exact + (np.log(nf / np.float32(max_exact)) / np.float32(math.log(MAX_DISTANCE / max_exact))
                         * np.float32(N_BUCKETS - max_exact)).astype(np.int32)
    large = np.minimum(large, N_BUCKETS - 1)
    bucket = np.where(n < max_exact, n, large)
    thr = []
    for b in range(1, N_BUCKETS):
        hit = np.nonzero(bucket >= b)[0]
        thr.append(int(hit[0]))
    assert bucket[-1] == N_BUCKETS - 1 and all(np.diff(bucket) >= 0)
    return tuple(thr)


_THR = _bucket_thresholds()


def _silu(x):
    return x * jax.nn.sigmoid(x)


def _inproj_kernel(x_ref, g_ref, w_ref, z_ref, h_ref):
    @pl.when(pl.program_id(1) == 0)
    def _():
        x = x_ref[...]
        ms = jnp.mean(x * x, axis=-1, keepdims=True)
        h_ref[...] = (x * lax.rsqrt(ms + EPS) * g_ref[...]).astype(BF16)

    z_ref[...] = jnp.dot(h_ref[...], w_ref[...].astype(BF16), preferred_element_type=F32)


def _inproj(x, g, w, tm, tn):
    m = x.shape[0]
    return pl.pallas_call(
        _inproj_kernel,
        grid=(m // tm, D_IN // tn),
        in_specs=[
            pl.BlockSpec((tm, D_MODEL), lambda i, j: (i, 0)),
            pl.BlockSpec((1, D_MODEL), lambda i, j: (0, 0)),
            pl.BlockSpec((D_MODEL, tn), lambda i, j: (0, j)),
        ],
        out_specs=pl.BlockSpec((tm, tn), lambda i, j: (i, j)),
        out_shape=jax.ShapeDtypeStruct((m, D_IN), F32),
        scratch_shapes=[pltpu.VMEM((tm, D_MODEL), BF16)],
        compiler_params=_params(("arbitrary", "arbitrary")),
        name="inproj",
    )(x, g, w)


def _rel_bias_value(rb_ref, h, dist):
    val = jnp.full(dist.shape, rb_ref[0, h], F32)
    for b in range(1, N_BUCKETS):
        val = jnp.where(dist >= _THR[b - 1], rb_ref[b, h], val)
    val = val - rb_ref[N_BUCKETS - 1, h]
    return jnp.where(dist < 0, NEG, val)


def _bias_kernel(rb_ref, sub_ref, diag_ref, *, t):
    h = pl.program_id(0)
    r = lax.broadcasted_iota(jnp.int32, (t, t), 0)
    c = lax.broadcasted_iota(jnp.int32, (t, t), 1)
    diag_ref[0] = _rel_bias_value(rb_ref, h, r - c)
    sub_ref[0] = _rel_bias_value(rb_ref, h, r - c + t)


def _prompt_bias(rel_bias, t):
    shp = jax.ShapeDtypeStruct((N_HEADS, t, t), F32)
    return pl.pallas_call(
        functools.partial(_bias_kernel, t=t),
        grid=(N_HEADS,),
        in_specs=[pl.BlockSpec(memory_space=pltpu.SMEM)],
        out_specs=[pl.BlockSpec((1, t, t), lambda h: (h, 0, 0))] * 2,
        out_shape=[shp, shp],
        compiler_params=_params(("arbitrary",)),
        name="prompt_bias",
    )(rel_bias)


def _sample_bias_kernel(rb_ref, past_ref, new_ref, *, ds):
    h = pl.program_id(0)
    npast = PAGE_SIZE * N_HEADS
    row = lax.broadcasted_iota(jnp.int32, (8, npast), 0)
    col = lax.broadcasted_iota(jnp.int32, (8, npast), 1)
    past_ref[...] = _rel_bias_value(rb_ref, h, row % ds + PAGE_SIZE - col // N_HEADS)
    row = lax.broadcasted_iota(jnp.int32, (8, 128), 0)
    col = lax.broadcasted_iota(jnp.int32, (8, 128), 1)
    new = _rel_bias_value(rb_ref, h, row % ds - col // N_HEADS)
    own = jnp.logical_and(col < ds * N_HEADS, col % N_HEADS == h)
    new_ref[...] = jnp.where(own, new, NEG)


def _sample_bias(rel_bias, ds):
    npast = PAGE_SIZE * N_HEADS
    return pl.pallas_call(
        functools.partial(_sample_bias_kernel, ds=ds),
        grid=(N_HEADS,),
        in_specs=[pl.BlockSpec(memory_space=pltpu.SMEM)],
        out_specs=[pl.BlockSpec((8, npast), lambda h: (h, 0)),
                   pl.BlockSpec((8, 128), lambda h: (h, 0))],
        out_shape=[jax.ShapeDtypeStruct((N_HEADS * 8, npast), F32),
                   jax.ShapeDtypeStruct((N_HEADS * 8, 128), F32)],
        compiler_params=_params(("arbitrary",)),
        name="sample_bias",
    )(rel_bias)


def _lambda(lamp_ref, lam_init):
    lp = lamp_ref[...]
    a = jnp.sum(lp[0:1] * lp[1:2], axis=1, keepdims=True)
    b = jnp.sum(lp[2:3] * lp[3:4], axis=1, keepdims=True)
    return jnp.exp(a) - jnp.exp(b) + lam_init


def _head_epilogue(o0, o1, lam, sg, ga, lam_init):
    o = o0 - lam * o1
    ms = jnp.mean(o * o, axis=-1, keepdims=True)
    o = o * lax.rsqrt(ms + EPS) * sg * (1.0 - lam_init)
    return o * _silu(ga)


def _pattn_kernel(q_ref, k_ref, v_ref, ga_ref, bsub_ref, bdiag_ref, lamp_ref, sg_ref,
                  o_ref, kt_ref, vx_ref, m_ref, acc_ref, *, t, seq, lam_init):
    i = pl.program_id(1)
    prep = 512

    @pl.when(i == 0)
    def _prep():
        def body(c, carry):
            r0 = pl.multiple_of(c * prep, prep)
            kt_ref[:, pl.ds(r0, prep)] = k_ref[pl.ds(r0, prep), :].T.astype(BF16)
            vx_ref[pl.ds(r0, prep), 0:DV] = v_ref[pl.ds(r0, prep), :].astype(BF16)
            return carry
        lax.fori_loop(0, seq // prep, body, 0)
        vx_ref[:, DV:2 * DV] = jnp.ones((seq, DV), BF16)

    q = q_ref[...] * (DK ** -0.5)
    lane = lax.broadcasted_iota(jnp.int32, (t, 2 * DK), 1)
    qa = jnp.where(lane < DK, q, 0.0).astype(BF16)
    qb = jnp.where(lane >= DK, q, 0.0).astype(BF16)
    qs = jnp.concatenate([qa, qb], axis=0)

    m_ref[...] = jnp.full(m_ref.shape, NEG, F32)
    acc_ref[...] = jnp.zeros(acc_ref.shape, F32)

    def step(c0, bias):
        s = jnp.dot(qs, kt_ref[:, pl.ds(c0, t)], preferred_element_type=F32)
        if bias is not None:
            s = (s.reshape(2, t, t) + bias[None]).reshape(2 * t, t)
        m_prev = m_ref[...]
        m_next = jnp.maximum(m_prev, jnp.max(s, axis=1, keepdims=True))
        alpha = jnp.exp(m_prev - m_next)
        p = jnp.exp(s - jnp.tile(m_next, (1, t // 128)))
        pv = jnp.dot(p.astype(BF16), vx_ref[pl.ds(c0, t), :], preferred_element_type=F32)
        acc_ref[...] = jnp.tile(alpha, (1, 2)) * acc_ref[...] + pv
        m_ref[...] = m_next

    def full_body(j, carry):
        step(pl.multiple_of(j * t, t), None)
        return carry
    lax.fori_loop(0, i - 1, full_body, 0)

    @pl.when(i >= 1)
    def _sub():
        step(pl.multiple_of((i - 1) * t, t), bsub_ref[0])

    step(pl.multiple_of(i * t, t), bdiag_ref[0])

    acc = acc_ref[...]
    o_all = acc[:, 0:DV] / acc[:, DV:2 * DV]
    lam = _lambda(lamp_ref, lam_init)
    o_ref[...] = _head_epilogue(o_all[0:t], o_all[t:2 * t], lam, sg_ref[...], ga_ref[...], lam_init)


def _prompt_attention(z, bsub, bdiag, lam_p, subln_g, t, lam_init):
    seq = z.shape[0]
    kern = functools.partial(_pattn_kernel, t=t, seq=seq, lam_init=lam_init)
    return pl.pallas_call(
        kern,
        grid=(N_HEADS, seq // t),
        in_specs=[
            pl.BlockSpec((t, DV), lambda h, i: (i, QB + h)),
            pl.BlockSpec((seq, DV), lambda h, i: (0, KB + h)),
            pl.BlockSpec((seq, DV), lambda h, i: (0, VB + h)),
            pl.BlockSpec((t, DV), lambda h, i: (i, GAB + h)),
            pl.BlockSpec((1, t, t), lambda h, i: (h, 0, 0)),
            pl.BlockSpec((1, t, t), lambda h, i: (h, 0, 0)),
            pl.BlockSpec((4, DK), lambda h, i: (0, 0)),
            pl.BlockSpec((1, DV), lambda h, i: (0, 0)),
        ],
        out_specs=pl.BlockSpec((t, DV), lambda h, i: (i, h)),
        out_shape=jax.ShapeDtypeStruct((seq, D_ATTN), F32),
        scratch_shapes=[
            pltpu.VMEM((DV, seq), BF16),
            pltpu.VMEM((seq, 2 * DV), BF16),
            pltpu.VMEM((2 * t, 128), F32),
            pltpu.VMEM((2 * t, 2 * DV), F32),
        ],
        compiler_params=_params(("arbitrary", "arbitrary")),
        name="prompt_attention",
    )(z, z, z, z, bsub, bdiag, lam_p, subln_g)


def _sattn_kernel(pt_ref, zq_ref, kn_ref, vn_ref, ck_ref, cv_ref, bpast_ref, bnew_ref, lamp_ref, sg_ref,
                  o_ref, kbuf, vbuf, sem, *, ds, n_pages, ppc, sub, lam_init):
    b = pl.program_id(0)
    nb = pl.num_programs(0)
    n_chunks = n_pages // ppc
    rows = PAGE_SIZE * N_HEADS
    nq = N_HEADS * 2 * ds

    def page_copies(req, chunk, slot):
        cps = []
        for pg in range(ppc):
            page = pt_ref[req, chunk * ppc + pg]
            cps.append(pltpu.make_async_copy(
                ck_ref.at[page], kbuf.at[slot, pl.ds(pg * rows, rows), :], sem.at[0, slot]))
            cps.append(pltpu.make_async_copy(
                cv_ref.at[page], vbuf.at[slot, pl.ds(pg * rows, rows), :], sem.at[1, slot]))
        return cps

    def start_chunk(req, chunk, slot):
        for cp in page_copies(req, chunk, slot):
            cp.start()

    def wait_chunk(req, chunk, slot):
        for cp in page_copies(req, chunk, slot):
            cp.wait()

    @pl.when(b == 0)
    def _prime():
        start_chunk(0, 0, 0)

    zq = zq_ref[0]
    lam = _lambda(lamp_ref, lam_init)
    row = lax.broadcasted_iota(jnp.int32, (2 * ds, 2 * DK), 0)
    lane = lax.broadcasted_iota(jnp.int32, (2 * ds, 2 * DK), 1)
    own_map = (lane >= DK) == (row >= ds)

    qrows = []
    for h in range(N_HEADS):
        qf = zq[:, (QB + h) * 128:(QB + h + 1) * 128] * (DK ** -0.5)
        qrows.append(jnp.where(own_map, jnp.concatenate([qf, qf], axis=0), 0.0))
    qall = jnp.concatenate(qrows, axis=0).astype(BF16)

    mrow = lax.broadcasted_iota(jnp.int32, (nq, 128), 0)
    mcol = lax.broadcasted_iota(jnp.int32, (nq, 128), 1)
    head_mask = jnp.where(mcol % N_HEADS == mrow // (2 * ds), 0.0, NEG)

    def block_softmax(s, vf):
        m = jnp.max(s, axis=1, keepdims=True)
        p = jnp.exp(s - m)
        l = jnp.sum(p, axis=1, keepdims=True)
        return m, l, jnp.dot(p.astype(BF16), vf, preferred_element_type=F32)

    def merge(parts):
        m = functools.reduce(jnp.maximum, [pm for pm, _, _ in parts])
        scale = [jnp.exp(pm - m) for pm, _, _ in parts]
        l = sum(sc * pl_ for sc, (_, pl_, _) in zip(scale, parts))
        acc = sum(sc * pa for sc, (_, _, pa) in zip(scale, parts))
        return m, l, acc

    def scores(kf):
        return lax.dot_general(qall, kf, (((1,), (1,)), ((), ())), preferred_element_type=F32)

    def chunk_body(c, carry):
        slot = (b * n_chunks + c) % 2
        nxt = c + 1

        @pl.when(nxt < n_chunks)
        def _():
            start_chunk(b, nxt, 1 - slot)

        @pl.when(jnp.logical_and(nxt == n_chunks, b + 1 < nb))
        def _():
            start_chunk(b + 1, 0, 1 - slot)

        wait_chunk(b, c, slot)
        n_sub = ppc * rows // sub
        parts = [carry]
        for sb in range(n_sub):
            kf = kbuf[slot, sb * sub:(sb + 1) * sub, :].astype(BF16)
            vf = vbuf[slot, sb * sub:(sb + 1) * sub, :].astype(BF16)
            s = scores(kf) + jnp.tile(head_mask, (1, sub // 128))
            if sb == n_sub - 1:
                tail = jnp.where(c == n_chunks - 1, bpast_ref[...], 0.0)
                s = jnp.concatenate([s[:, :sub - rows], s[:, sub - rows:] + tail], axis=1)
            parts.append(block_softmax(s, vf))
        return merge(parts)

    init = (jnp.full((nq, 1), NEG, F32), jnp.zeros((nq, 1), F32), jnp.zeros((nq, DV), F32))
    carry = lax.fori_loop(0, n_chunks, chunk_body, init)

    pad = jnp.zeros((128 - ds * N_HEADS, DV), F32)
    kn = jnp.concatenate([kn_ref[0], pad], axis=0).astype(BF16)
    vn = jnp.concatenate([vn_ref[0], pad], axis=0).astype(BF16)
    _, l, acc = merge([carry, block_softmax(scores(kn) + bnew_ref[...], vn)])
    o_all = acc / l
    for h in range(N_HEADS):
        r0 = h * 2 * ds
        ga = zq[:, (GAB + h) * 128:(GAB + h + 1) * 128]
        o_ref[0, :, h * DV:(h + 1) * DV] = _head_epilogue(
            o_all[r0:r0 + ds], o_all[r0 + ds:r0 + 2 * ds], lam, sg_ref[...], ga, lam_init)


def _sample_attention(zs, cache_k, cache_v, page_table, bpast, bnew, lam_p, subln_g, ds, lam_init,
                      ppc=8, sub=2048):
    db, n_pages = page_table.shape
    rows = PAGE_SIZE * N_HEADS
    n_pool = cache_k.shape[0]
    nq = N_HEADS * 2 * ds
    assert 2 * ds == 8 and n_pages % ppc == 0 and (ppc * rows) % sub == 0 and sub >= rows
    ck = cache_k.reshape(n_pool, rows, 2 * DK)
    cv = cache_v.reshape(n_pool, rows, DV)
    kn = zs[:, D_ATTN:2 * D_ATTN].reshape(db, ds * N_HEADS, 2 * DK)
    vn = zs[:, 2 * D_ATTN:3 * D_ATTN].reshape(db, ds * N_HEADS, DV)
    kern = functools.partial(_sattn_kernel, ds=ds, n_pages=n_pages, ppc=ppc, sub=sub, lam_init=lam_init)
    grid_spec = pltpu.PrefetchScalarGridSpec(
        num_scalar_prefetch=1,
        grid=(db,),
        in_specs=[
            pl.BlockSpec((1, ds, D_IN), lambda b, pt: (b, 0, 0)),
            pl.BlockSpec((1, ds * N_HEADS, 2 * DK), lambda b, pt: (b, 0, 0)),
            pl.BlockSpec((1, ds * N_HEADS, DV), lambda b, pt: (b, 0, 0)),
            pl.BlockSpec(memory_space=pl.ANY),
            pl.BlockSpec(memory_space=pl.ANY),
            pl.BlockSpec((nq, rows), lambda b, pt: (0, 0)),
            pl.BlockSpec((nq, 128), lambda b, pt: (0, 0)),
            pl.BlockSpec((4, DK), lambda b, pt: (0, 0)),
            pl.BlockSpec((1, DV), lambda b, pt: (0, 0)),
        ],
        out_specs=pl.BlockSpec((1, ds, D_ATTN), lambda b, pt: (b, 0, 0)),
        scratch_shapes=[
            pltpu.VMEM((2, ppc * rows, 2 * DK), F32),
            pltpu.VMEM((2, ppc * rows, DV), F32),
            pltpu.SemaphoreType.DMA((2, 2)),
        ],
    )
    return pl.pallas_call(
        kern,
        grid_spec=grid_spec,
        out_shape=jax.ShapeDtypeStruct((db, ds, D_ATTN), F32),
        compiler_params=_params(("arbitrary",)),
        name="sample_attention",
    )(page_table, zs.reshape(db, ds, D_IN), kn, vn, ck, cv, bpast, bnew, lam_p, subln_g)


def _ln_swish(y, lg, lb):
    mu = jnp.mean(y, axis=-1, keepdims=True)
    yc = y - mu
    var = jnp.mean(yc * yc, axis=-1, keepdims=True)
    return _silu(yc * lax.rsqrt(var + EPS) * lg + lb)


def _pconv_kernel(ca_ref, cb_ref, wdw_ref, bdw_ref, lg_ref, lb_ref, t_ref, nb_ref, ext_ref, sh_ref, wb_ref,
                  y_ref, *, ts, rc):
    i = pl.program_id(0)
    pad = 32
    first = pad - HALO

    @pl.when(i == 0)
    def _():
        ext_ref[0:pad, :] = jnp.zeros((pad, D_CONV), F32)

    ext_ref[pad:pad + ts, :] = ca_ref[...] * jax.nn.sigmoid(cb_ref[...])
    n_sh = sh_ref.shape[1]
    for b in range(1, 8):
        sh_ref[b - 1] = ext_ref[b:b + n_sh, :]
    for j in range(CONV_WIDTH):
        wb_ref[j] = jnp.broadcast_to(wdw_ref[j:j + 1, :], (8, D_CONV))

    def chunk(r, carry):
        r0 = pl.multiple_of(r * rc, rc)
        accs = [jnp.zeros((8, D_CONV), F32) for _ in range(rc // 8)]
        for j in range(CONV_WIDTH):
            a8, b = (first + j) // 8 * 8, (first + j) % 8
            src = ext_ref if b == 0 else sh_ref.at[b - 1]
            w8 = wb_ref[j]
            for g in range(rc // 8):
                accs[g] = accs[g] + w8 * src[pl.ds(r0 + a8 + 8 * g, 8), :]
        y_ref[pl.ds(r0, rc), :] = jnp.concatenate(accs, axis=0)
        return carry

    lax.fori_loop(0, ts // rc, chunk, 0)
    y = y_ref[...] + bdw_ref[...]
    t_ref[...] = _ln_swish(y, lg_ref[...], lb_ref[...]).astype(BF16)
    nb_ref[...] = ext_ref[pad + ts - HALO:pad + ts, :]
    ext_ref[0:pad, :] = ext_ref[ts:ts + pad, :]


def _prompt_conv(z, w_dw, b_dw, ln_g, ln_b, ts=512, rc=16):
    seq = z.shape[0]
    vec = pl.BlockSpec((1, D_CONV), lambda i: (0, 0))
    return pl.pallas_call(
        functools.partial(_pconv_kernel, ts=ts, rc=rc),
        grid=(seq // ts,),
        in_specs=[
            pl.BlockSpec((ts, D_CONV), lambda i: (i, CA_BLK)),
            pl.BlockSpec((ts, D_CONV), lambda i: (i, CB_BLK)),
            pl.BlockSpec((CONV_WIDTH, D_CONV), lambda i: (0, 0)),
            vec, vec, vec,
        ],
        out_specs=[
            pl.BlockSpec((ts, D_CONV), lambda i: (i, 0)),
            pl.BlockSpec((HALO, D_CONV), lambda i: (0, 0)),
        ],
        out_shape=[
            jax.ShapeDtypeStruct((seq, D_CONV), BF16),
            jax.ShapeDtypeStruct((HALO, D_CONV), F32),
        ],
        scratch_shapes=[pltpu.VMEM((ts + 32, D_CONV), F32),
                        pltpu.VMEM((7, ts + 24, D_CONV), F32),
                        pltpu.VMEM((CONV_WIDTH, 8, D_CONV), F32),
                        pltpu.VMEM((ts, D_CONV), F32)],
        compiler_params=_params(("arbitrary",)),
        name="prompt_conv",
    )(z, z, w_dw, b_dw, ln_g, ln_b)


def _sconv_kernel(z_ref, st_ref, wdw_ref, bdw_ref, lg_ref, lb_ref, t_ref, nb_ref, ext_ref, *, ds):
    ca = z_ref[0, :, CA_BLK * D_CONV:(CA_BLK + 1) * D_CONV]
    cb = z_ref[0, :, CB_BLK * D_CONV:(CB_BLK + 1) * D_CONV]
    ext_ref[0:HALO, :] = st_ref[0]
    ext_ref[HALO:HALO + ds, :] = ca * jax.nn.sigmoid(cb)
    acc = jnp.zeros((ds, D_CONV), F32)
    for j in range(CONV_WIDTH):
        acc = acc + wdw_ref[j:j + 1, :] * ext_ref[j:j + ds, :]
    y = acc + bdw_ref[...]
    t_ref[0] = _ln_swish(y, lg_ref[...], lb_ref[...])
    nb_ref[0] = ext_ref[ds:ds + HALO, :]


def _sample_conv(zs3, state, w_dw, b_dw, ln_g, ln_b):
    db, ds, _ = zs3.shape
    vec = pl.BlockSpec((1, D_CONV), lambda b: (0, 0))
    return pl.pallas_call(
        functools.partial(_sconv_kernel, ds=ds),
        grid=(db,),
        in_specs=[
            pl.BlockSpec((1, ds, D_IN), lambda b: (b, 0, 0)),
            pl.BlockSpec((1, HALO, D_CONV), lambda b: (b, 0, 0)),
            pl.BlockSpec((CONV_WIDTH, D_CONV), lambda b: (0, 0)),
            vec, vec, vec,
        ],
        out_specs=[
            pl.BlockSpec((1, ds, D_CONV), lambda b: (b, 0, 0)),
            pl.BlockSpec((1, HALO, D_CONV), lambda b: (b, 0, 0)),
        ],
        out_shape=[
            jax.ShapeDtypeStruct((db, ds, D_CONV), F32),
            jax.ShapeDtypeStruct((db, HALO, D_CONV), F32),
        ],
        scratch_shapes=[pltpu.VMEM((HALO + 8, D_CONV), F32)],
        compiler_params=_params(("arbitrary",)),
        name="sample_conv",
    )(zs3, state, w_dw, b_dw, ln_g, ln_b)


def _outproj_kernel(x_ref, ya_ref, t_ref, gc_ref, wpw_ref, bpw_ref, wout_ref, fg_ref, y_ref):
    yc = jnp.dot(t_ref[...].astype(BF16), wpw_ref[...], preferred_element_type=F32) + bpw_ref[...]
    yc = yc * _silu(gc_ref[...])
    y = jnp.dot(ya_ref[...].astype(BF16), wout_ref[0:D_ATTN, :], preferred_element_type=F32)
    y = y + jnp.dot(yc.astype(BF16), wout_ref[D_ATTN:D_MODEL, :], preferred_element_type=F32)
    r = x_ref[...] + y
    ms = jnp.mean(r * r, axis=-1, keepdims=True)
    y_ref[...] = r * lax.rsqrt(ms + EPS) * fg_ref[...]


def _outproj(x, y_attn, t, z, w_pw_bf16, b_pw, w_out_bf16, final_g, tm):
    m = x.shape[0]
    return pl.pallas_call(
        _outproj_kernel,
        grid=(m // tm,),
        in_specs=[
            pl.BlockSpec((tm, D_MODEL), lambda i: (i, 0)),
            pl.BlockSpec((tm, D_ATTN), lambda i: (i, 0)),
            pl.BlockSpec((tm, D_CONV), lambda i: (i, 0)),
            pl.BlockSpec((tm, D_CONV), lambda i: (i, GC_BLK)),
            pl.BlockSpec((D_CONV, D_CONV), lambda i: (0, 0)),
            pl.BlockSpec((1, D_CONV), lambda i: (0, 0)),
            pl.BlockSpec((D_MODEL, D_MODEL), lambda i: (0, 0)),
            pl.BlockSpec((1, D_MODEL), lambda i: (0, 0)),
        ],
        out_specs=pl.BlockSpec((tm, D_MODEL), lambda i: (i, 0)),
        out_shape=jax.ShapeDtypeStruct((m, D_MODEL), F32),
        compiler_params=_params(("arbitrary",)),
        name="outproj",
    )(x, y_attn, t, z, w_pw_bf16, b_pw, w_out_bf16, final_g)


def kernel(x_prompt, x_sample, cache_k, cache_v, state_conv, page_table, norm_g, w_in, lam_p, subln_g,
           w_dw, b_dw, conv_ln_g, conv_ln_b, w_pw, b_pw, w_out, rel_bias, final_g):
    depth = w_in.shape[0]
    assert depth == 1, "single-layer step"
    bsz, seq, _ = x_prompt.shape
    assert bsz == 1
    db, ds, _ = x_sample.shape
    lam_init = 0.8 - 0.6 * math.exp(-0.3 * 0)
    t_attn = 512

    w_in_l = w_in[0]
    w_pw_b = w_pw[0].astype(BF16)
    w_out_b = w_out[0].astype(BF16)
    g = norm_g[0][None, :]
    sg = subln_g[0][None, :]
    bdw, lg, lb, bpw = b_dw[0][None, :], conv_ln_g[0][None, :], conv_ln_b[0][None, :], b_pw[0][None, :]
    fg = final_g[None, :]

    xp = x_prompt.reshape(seq, D_MODEL)
    xs = x_sample.reshape(db * ds, D_MODEL)

    zp = _inproj(xp, g, w_in_l, tm=1024, tn=512)
    bsub, bdiag = _prompt_bias(rel_bias, t_attn)
    ya_p = _prompt_attention(zp, bsub, bdiag, lam_p[0], sg, t_attn, lam_init)
    t_p, nc_p = _prompt_conv(zp, w_dw[0], bdw, lg, lb)
    y_p = _outproj(xp, ya_p, t_p, zp, w_pw_b, bpw, w_out_b, fg, tm=256)

    zs = _inproj(xs, g, w_in_l, tm=db * ds, tn=512)
    bpast, bnew = _sample_bias(rel_bias, ds)
    ya_s = _sample_attention(zs, cache_k[0], cache_v[0], page_table, bpast, bnew, lam_p[0], sg, ds, lam_init)
    zs3 = zs.reshape(db, ds, D_IN)
    t_s, nc_s = _sample_conv(zs3, state_conv[0], w_dw[0], bdw, lg, lb)
    y_s = _outproj(xs, ya_s.reshape(db * ds, D_ATTN), t_s.reshape(db * ds, D_CONV), zs,
                   w_pw_b, bpw, w_out_b, fg, tm=db * ds)

    return (
        y_p.reshape(1, seq, D_MODEL),
        y_s.reshape(db, ds, D_MODEL),
        zp[:, D_ATTN:2 * D_ATTN].reshape(1, 1, seq, N_HEADS, 2 * DK),
        zp[:, 2 * D_ATTN:3 * D_ATTN].reshape(1, 1, seq, N_HEADS, DV),
        nc_p.reshape(1, 1, HALO, D_CONV),
        zs[:, D_ATTN:2 * D_ATTN].reshape(1, db, ds, N_HEADS, 2 * DK),
        zs[:, 2 * D_ATTN:3 * D_ATTN].reshape(1, db, ds, N_HEADS, DV),
        nc_s.reshape(1, db, HALO, D_CONV),
    )
```

```python
import functools
import math

import jax
import jax.numpy as jnp
import numpy as np
from jax import lax
from jax.experimental import pallas as pl
from jax.experimental.pallas import tpu as pltpu

F32 = jnp.float32
BF16 = jnp.bfloat16

N_HEADS = 8
DV = 128
DK = 64
D_ATTN = N_HEADS * DV
D_CONV = 1024
D_MODEL = D_ATTN + D_CONV
D_IN = 4 * D_ATTN + 3 * D_CONV
CONV_WIDTH = 31
HALO = CONV_WIDTH - 1
N_BUCKETS = 32
MAX_DISTANCE = 128
PAGE_SIZE = 128
EPS = 1e-6
NEG = -1e30
LOG2E = math.log2(math.e)

QB, KB, VB, GAB = 0, N_HEADS, 2 * N_HEADS, 3 * N_HEADS
CA_BLK, CB_BLK, GC_BLK = 4, 5, 6

VMEM_LIMIT = 56 * 1024 * 1024


def _params(sem, vmem=VMEM_LIMIT):
    return pltpu.CompilerParams(dimension_semantics=sem, vmem_limit_bytes=vmem)


def _bucket_thresholds():
    n = np.arange(0, MAX_DISTANCE + 1)
    max_exact = N_BUCKETS // 2
    nf = np.maximum(n, 1).astype(np.float32)
    large = max_exact + (np.log(nf / np.float32(max_exact)) / np.float32(math.log(MAX_DISTANCE / max_exact))
                         * np.float32(N_BUCKETS - max_exact)).astype(np.int32)
    large = np.minimum(large, N_BUCKETS - 1)
    bucket = np.where(n < max_exact, n, large)
    thr = []
    for b in range(1, N_BUCKETS):
        hit = np.nonzero(bucket >= b)[0]
        thr.append(int(hit[0]))
    assert bucket[-1] == N_BUCKETS - 1 and all(np.diff(bucket) >= 0)
    return tuple(thr)


_THR = _bucket_thresholds()


def _silu(x):
    return x * jax.nn.sigmoid(x)


def _inproj_kernel(x_ref, g_ref, w_ref, z_ref, h_ref):
    @pl.when(pl.program_id(1) == 0)
    def _():
        x = x_ref[...]
        ms = jnp.mean(x * x, axis=-1, keepdims=True)
        h_ref[...] = (x * lax.rsqrt(ms + EPS) * g_ref[...]).astype(BF16)

    z_ref[...] = jnp.dot(h_ref[...], w_ref[...], preferred_element_type=F32)


def _inproj(x, g, w, tm, tn):
    m = x.shape[0]
    return pl.pallas_call(
        _inproj_kernel,
        grid=(m // tm, D_IN // tn),
        in_specs=[
            pl.BlockSpec((tm, D_MODEL), lambda i, j: (i, 0)),
            pl.BlockSpec((1, D_MODEL), lambda i, j: (0, 0)),
            pl.BlockSpec((D_MODEL, tn), lambda i, j: (0, j)),
        ],
        out_specs=pl.BlockSpec((tm, tn), lambda i, j: (i, j)),
        out_shape=jax.ShapeDtypeStruct((m, D_IN), F32),
        scratch_shapes=[pltpu.VMEM((tm, D_MODEL), BF16)],
        compiler_params=_params(("arbitrary", "arbitrary")),
        name="inproj",
    )(x, g, w)


def _rel_bias_value(rb_ref, h, dist):
    val = jnp.full(dist.shape, rb_ref[0, h], F32)
    for b in range(1, N_BUCKETS):
        val = jnp.where(dist >= _THR[b - 1], rb_ref[b, h], val)
    val = val - rb_ref[N_BUCKETS - 1, h]
    return jnp.where(dist < 0, NEG, val)


def _bias_kernel(rb_ref, sub_ref, diag_ref, *, t):
    h = pl.program_id(0)
    r = lax.broadcasted_iota(jnp.int32, (t, t), 0)
    c = lax.broadcasted_iota(jnp.int32, (t, t), 1)
    diag_ref[0] = _rel_bias_value(rb_ref, h, r - c) * LOG2E
    sub_ref[0] = _rel_bias_value(rb_ref, h, r - c + t) * LOG2E


def _prompt_bias(rel_bias, t):
    shp = jax.ShapeDtypeStruct((N_HEADS, t, t), F32)
    return pl.pallas_call(
        functools.partial(_bias_kernel, t=t),
        grid=(N_HEADS,),
        in_specs=[pl.BlockSpec(memory_space=pltpu.SMEM)],
        out_specs=[pl.BlockSpec((1, t, t), lambda h: (h, 0, 0))] * 2,
        out_shape=[shp, shp],
        compiler_params=_params(("arbitrary",)),
        name="prompt_bias",
    )(rel_bias)


def _sample_bias_kernel(rb_ref, past_ref, new_ref, *, ds):
    h = pl.program_id(0)
    npast = PAGE_SIZE * N_HEADS
    row = lax.broadcasted_iota(jnp.int32, (8, npast), 0)
    col = lax.broadcasted_iota(jnp.int32, (8, npast), 1)
    past_ref[...] = _rel_bias_value(rb_ref, h, row % ds + PAGE_SIZE - col // N_HEADS)
    row = lax.broadcasted_iota(jnp.int32, (8, 128), 0)
    col = lax.broadcasted_iota(jnp.int32, (8, 128), 1)
    new = _rel_bias_value(rb_ref, h, row % ds - col // N_HEADS)
    own = jnp.logical_and(col < ds * N_HEADS, col % N_HEADS == h)
    new_ref[...] = jnp.where(own, new, NEG)


def _sample_bias(rel_bias, ds):
    npast = PAGE_SIZE * N_HEADS
    return pl.pallas_call(
        functools.partial(_sample_bias_kernel, ds=ds),
        grid=(N_HEADS,),
        in_specs=[pl.BlockSpec(memory_space=pltpu.SMEM)],
        out_specs=[pl.BlockSpec((8, npast), lambda h: (h, 0)),
                   pl.BlockSpec((8, 128), lambda h: (h, 0))],
        out_shape=[jax.ShapeDtypeStruct((N_HEADS * 8, npast), F32),
                   jax.ShapeDtypeStruct((N_HEADS * 8, 128), F32)],
        compiler_params=_params(("arbitrary",)),
        name="sample_bias",
    )(rel_bias)


def _lambda(lamp_ref, lam_init):
    lp = lamp_ref[...]
    a = jnp.sum(lp[0:1] * lp[1:2], axis=1, keepdims=True)
    b = jnp.sum(lp[2:3] * lp[3:4], axis=1, keepdims=True)
    return jnp.exp(a) - jnp.exp(b) + lam_init


def _head_epilogue(o0, o1, lam, sg, ga, lam_init):
    o = o0 - lam * o1
    ms = jnp.mean(o * o, axis=-1, keepdims=True)
    o = o * lax.rsqrt(ms + EPS) * sg * (1.0 - lam_init)
    return o * _silu(ga)


def _pattn_kernel(q_ref, k_ref, v_ref, ga_ref, bsub_ref, bdiag_ref, lamp_ref, sg_ref,
                  o_ref, kt_ref, vx_ref, m_ref, acc_ref, *, t, seq, group, lam_init):
    i = pl.program_id(1)
    prep = 512

    @pl.when(i == 0)
    def _prep():
        def body(c, carry):
            r0 = pl.multiple_of(c * prep, prep)
            kt_ref[:, pl.ds(r0, prep)] = k_ref[pl.ds(r0, prep), :].T.astype(BF16)
            vx_ref[pl.ds(r0, prep), 0:DV] = v_ref[pl.ds(r0, prep), :].astype(BF16)
            return carry
        lax.fori_loop(0, seq // prep, body, 0)
        vx_ref[:, DV:2 * DV] = jnp.ones((seq, DV), BF16)

    q = q_ref[...] * (DK ** -0.5 * LOG2E)
    lane = lax.broadcasted_iota(jnp.int32, (t, 2 * DK), 1)
    qa = jnp.where(lane < DK, q, 0.0).astype(BF16)
    qb = jnp.where(lane >= DK, q, 0.0).astype(BF16)
    qs = jnp.concatenate([qa, qb], axis=0)

    m_ref[...] = jnp.full(m_ref.shape, NEG, F32)
    acc_ref[...] = jnp.zeros(acc_ref.shape, F32)

    def scores(blk, bias):
        c0 = pl.multiple_of(blk * t, t)
        s = jnp.dot(qs, kt_ref[:, pl.ds(c0, t)], preferred_element_type=F32)
        if bias is not None:
            s = (s.reshape(2, t, t) + bias[None]).reshape(2 * t, t)
        return s

    def update(blk, s):
        c0 = pl.multiple_of(blk * t, t)
        m_prev = m_ref[...]
        m_next = jnp.maximum(m_prev, jnp.max(s, axis=1, keepdims=True))
        alpha = jnp.exp2(m_prev - m_next)
        p = jnp.exp2(s - jnp.tile(m_next, (1, t // 128)))
        pv = jnp.dot(p.astype(BF16), vx_ref[pl.ds(c0, t), :], preferred_element_type=F32)
        acc_ref[...] = jnp.tile(alpha, (1, 2)) * acc_ref[...] + pv
        m_ref[...] = m_next

    def steps(blocks):
        ss = [scores(blk, bias) for blk, bias in blocks]
        for (blk, _), s in zip(blocks, ss):
            update(blk, s)

    n_full = jnp.maximum(i - 1, 0)

    def group_body(jj, carry):
        steps([(group * jj + g, None) for g in range(group)])
        return carry
    lax.fori_loop(0, n_full // group, group_body, 0)

    @pl.when(i == 0)
    def _first_tile():
        steps([(0, bdiag_ref[0])])

    for rem in range(group):
        @pl.when(jnp.logical_and(i >= 1, n_full % group == rem))
        def _tail():
            left = [(i - 1 - rem + g, None) for g in range(rem)]
            steps(left + [(i - 1, bsub_ref[0]), (i, bdiag_ref[0])])

    acc = acc_ref[...]
    o_all = acc[:, 0:DV] / acc[:, DV:2 * DV]
    lam = _lambda(lamp_ref, lam_init)
    o_ref[...] = _head_epilogue(o_all[0:t], o_all[t:2 * t], lam, sg_ref[...], ga_ref[...], lam_init)


def _prompt_attention(z, bsub, bdiag, lam_p, subln_g, t, lam_init, group=4):
    seq = z.shape[0]
    kern = functools.partial(_pattn_kernel, t=t, seq=seq, group=group, lam_init=lam_init)
    return pl.pallas_call(
        kern,
        grid=(N_HEADS, seq // t),
        in_specs=[
            pl.BlockSpec((t, DV), lambda h, i: (i, QB + h)),
            pl.BlockSpec((seq, DV), lambda h, i: (0, KB + h)),
            pl.BlockSpec((seq, DV), lambda h, i: (0, VB + h)),
            pl.BlockSpec((t, DV), lambda h, i: (i, GAB + h)),
            pl.BlockSpec((1, t, t), lambda h, i: (h, 0, 0)),
            pl.BlockSpec((1, t, t), lambda h, i: (h, 0, 0)),
            pl.BlockSpec((4, DK), lambda h, i: (0, 0)),
            pl.BlockSpec((1, DV), lambda h, i: (0, 0)),
        ],
        out_specs=pl.BlockSpec((t, DV), lambda h, i: (i, h)),
        out_shape=jax.ShapeDtypeStruct((seq, D_ATTN), F32),
        scratch_shapes=[
            pltpu.VMEM((DV, seq), BF16),
            pltpu.VMEM((seq, 2 * DV), BF16),
            pltpu.VMEM((2 * t, 128), F32),
            pltpu.VMEM((2 * t, 2 * DV), F32),
        ],
        compiler_params=_params(("arbitrary", "arbitrary")),
        name="prompt_attention",
    )(z, z, z, z, bsub, bdiag, lam_p, subln_g)


def _sattn_kernel(pt_ref, zq_ref, kn_ref, vn_ref, ck_ref, cv_ref, bpast_ref, bnew_ref, lamp_ref, sg_ref,
                  o_ref, kbuf, vbuf, sem, *, ds, n_pages, ppc, sub, lam_init):
    b = pl.program_id(0)
    nb = pl.num_programs(0)
    n_chunks = n_pages // ppc
    rows = PAGE_SIZE * N_HEADS
    nq = N_HEADS * 2 * ds

    def page_copies(req, chunk, slot):
        cps = []
        for pg in range(ppc):
            page = pt_ref[req, chunk * ppc + pg]
            cps.append(pltpu.make_async_copy(
                ck_ref.at[page], kbuf.at[slot, pl.ds(pg * rows, rows), :], sem.at[0, slot]))
            cps.append(pltpu.make_async_copy(
                cv_ref.at[page], vbuf.at[slot, pl.ds(pg * rows, rows), :], sem.at[1, slot]))
        return cps

    def start_chunk(req, chunk, slot):
        for cp in page_copies(req, chunk, slot):
            cp.start()

    def wait_chunk(req, chunk, slot):
        for cp in page_copies(req, chunk, slot):
            cp.wait()

    n_slots = kbuf.shape[0]
    ahead = n_slots - 1

    @pl.when(b == 0)
    def _prime():
        for g in range(ahead):
            start_chunk(g // n_chunks, g % n_chunks, g % n_slots)

    zq = zq_ref[0]
    lam = _lambda(lamp_ref, lam_init)
    row = lax.broadcasted_iota(jnp.int32, (2 * ds, 2 * DK), 0)
    lane = lax.broadcasted_iota(jnp.int32, (2 * ds, 2 * DK), 1)
    own_map = (lane >= DK) == (row >= ds)

    qrows = []
    for h in range(N_HEADS):
        qf = zq[:, (QB + h) * 128:(QB + h + 1) * 128] * (DK ** -0.5)
        qrows.append(jnp.where(own_map, jnp.concatenate([qf, qf], axis=0), 0.0))
    qall = jnp.concatenate(qrows, axis=0).astype(BF16)

    mrow = lax.broadcasted_iota(jnp.int32, (nq, 128), 0)
    mcol = lax.broadcasted_iota(jnp.int32, (nq, 128), 1)
    head_mask = jnp.where(mcol % N_HEADS == mrow // (2 * ds), 0.0, NEG)

    def block_softmax(s, vf):
        m = jnp.max(s, axis=1, keepdims=True)
        p = jnp.exp(s - m)
        l = jnp.sum(p, axis=1, keepdims=True)
        return m, l, jnp.dot(p.astype(BF16), vf, preferred_element_type=F32)

    def merge(parts):
        m = functools.reduce(jnp.maximum, [pm for pm, _, _ in parts])
        scale = [jnp.exp(pm - m) for pm, _, _ in parts]
        l = sum(sc * pl_ for sc, (_, pl_, _) in zip(scale, parts))
        acc = sum(sc * pa for sc, (_, _, pa) in zip(scale, parts))
        return m, l, acc

    def scores(kf):
        return lax.dot_general(qall, kf, (((1,), (1,)), ((), ())), preferred_element_type=F32)

    def chunk_body(c, carry):
        g = b * n_chunks + c
        slot = g % n_slots
        nxt = c + ahead
        nxt_req = b + nxt // n_chunks

        @pl.when(nxt_req < nb)
        def _():
            start_chunk(nxt_req, nxt % n_chunks, (g + ahead) % n_slots)

        wait_chunk(b, c, slot)
        n_sub = ppc * rows // sub
        parts = [carry]
        for sb in range(n_sub):
            kf = kbuf[slot, sb * sub:(sb + 1) * sub, :].astype(BF16)
            vf = vbuf[slot, sb * sub:(sb + 1) * sub, :].astype(BF16)
            s = scores(kf) + jnp.tile(head_mask, (1, sub // 128))
            if sb == n_sub - 1:
                tail = jnp.where(c == n_chunks - 1, bpast_ref[...], 0.0)
                s = jnp.concatenate([s[:, :sub - rows], s[:, sub - rows:] + tail], axis=1)
            parts.append(block_softmax(s, vf))
        return merge(parts)

    init = (jnp.full((nq, 1), NEG, F32), jnp.zeros((nq, 1), F32), jnp.zeros((nq, DV), F32))
    carry = lax.fori_loop(0, n_chunks, chunk_body, init)

    pad = jnp.zeros((128 - ds * N_HEADS, DV), F32)
    kn = jnp.concatenate([kn_ref[0], pad], axis=0).astype(BF16)
    vn = jnp.concatenate([vn_ref[0], pad], axis=0).astype(BF16)
    _, l, acc = merge([carry, block_softmax(scores(kn) + bnew_ref[...], vn)])
    o_all = acc / l
    for h in range(N_HEADS):
        r0 = h * 2 * ds
        ga = zq[:, (GAB + h) * 128:(GAB + h + 1) * 128]
        o_ref[0, :, h * DV:(h + 1) * DV] = _head_epilogue(
            o_all[r0:r0 + ds], o_all[r0 + ds:r0 + 2 * ds], lam, sg_ref[...], ga, lam_init)


def _sample_attention(zs, cache_k, cache_v, page_table, bpast, bnew, lam_p, subln_g, ds, lam_init,
                      ppc=8, sub=2048, n_slots=3):
    db, n_pages = page_table.shape
    rows = PAGE_SIZE * N_HEADS
    n_pool = cache_k.shape[0]
    nq = N_HEADS * 2 * ds
    assert 2 * ds == 8 and n_pages % ppc == 0 and (ppc * rows) % sub == 0 and sub >= rows
    ck = cache_k.reshape(n_pool, rows, 2 * DK)
    cv = cache_v.reshape(n_pool, rows, DV)
    kn = zs[:, D_ATTN:2 * D_ATTN].reshape(db, ds * N_HEADS, 2 * DK)
    vn = zs[:, 2 * D_ATTN:3 * D_ATTN].reshape(db, ds * N_HEADS, DV)
    kern = functools.partial(_sattn_kernel, ds=ds, n_pages=n_pages, ppc=ppc, sub=sub, lam_init=lam_init)
    grid_spec = pltpu.PrefetchScalarGridSpec(
        num_scalar_prefetch=1,
        grid=(db,),
        in_specs=[
            pl.BlockSpec((1, ds, D_IN), lambda b, pt: (b, 0, 0)),
            pl.BlockSpec((1, ds * N_HEADS, 2 * DK), lambda b, pt: (b, 0, 0)),
            pl.BlockSpec((1, ds * N_HEADS, DV), lambda b, pt: (b, 0, 0)),
            pl.BlockSpec(memory_space=pl.ANY),
            pl.BlockSpec(memory_space=pl.ANY),
            pl.BlockSpec((nq, rows), lambda b, pt: (0, 0)),
            pl.BlockSpec((nq, 128), lambda b, pt: (0, 0)),
            pl.BlockSpec((4, DK), lambda b, pt: (0, 0)),
            pl.BlockSpec((1, DV), lambda b, pt: (0, 0)),
        ],
        out_specs=pl.BlockSpec((1, ds, D_ATTN), lambda b, pt: (b, 0, 0)),
        scratch_shapes=[
            pltpu.VMEM((n_slots, ppc * rows, 2 * DK), F32),
            pltpu.VMEM((n_slots, ppc * rows, DV), F32),
            pltpu.SemaphoreType.DMA((2, n_slots)),
        ],
    )
    return pl.pallas_call(
        kern,
        grid_spec=grid_spec,
        out_shape=jax.ShapeDtypeStruct((db, ds, D_ATTN), F32),
        compiler_params=_params(("arbitrary",)),
        name="sample_attention",
    )(page_table, zs.reshape(db, ds, D_IN), kn, vn, ck, cv, bpast, bnew, lam_p, subln_g)


def _ln_swish(y, lg, lb):
    mu = jnp.mean(y, axis=-1, keepdims=True)
    yc = y - mu
    var = jnp.mean(yc * yc, axis=-1, keepdims=True)
    return _silu(yc * lax.rsqrt(var + EPS) * lg + lb)


def _pconv_kernel(ca_ref, cb_ref, wdw_ref, bdw_ref, lg_ref, lb_ref, t_ref, nb_ref, ext_ref, sh_ref, wb_ref,
                  y_ref, *, ts, rc):
    i = pl.program_id(0)
    pad = 32
    first = pad - HALO

    @pl.when(i == 0)
    def _():
        ext_ref[0:pad, :] = jnp.zeros((pad, D_CONV), F32)

    ext_ref[pad:pad + ts, :] = ca_ref[...] * jax.nn.sigmoid(cb_ref[...])
    n_sh = sh_ref.shape[1]
    for b in range(1, 8):
        sh_ref[b - 1] = ext_ref[b:b + n_sh, :]
    for j in range(CONV_WIDTH):
        wb_ref[j] = jnp.broadcast_to(wdw_ref[j:j + 1, :], (8, D_CONV))

    def chunk(r, carry):
        r0 = pl.multiple_of(r * rc, rc)
        accs = [jnp.zeros((8, D_CONV), F32) for _ in range(rc // 8)]
        for j in range(CONV_WIDTH):
            a8, b = (first + j) // 8 * 8, (first + j) % 8
            src = ext_ref if b == 0 else sh_ref.at[b - 1]
            w8 = wb_ref[j]
            for g in range(rc // 8):
                accs[g] = accs[g] + w8 * src[pl.ds(r0 + a8 + 8 * g, 8), :]
        y_ref[pl.ds(r0, rc), :] = jnp.concatenate(accs, axis=0)
        return carry

    lax.fori_loop(0, ts // rc, chunk, 0)
    y = y_ref[...] + bdw_ref[...]
    t_ref[...] = _ln_swish(y, lg_ref[...], lb_ref[...]).astype(BF16)
    nb_ref[...] = ext_ref[pad + ts - HALO:pad + ts, :]
    ext_ref[0:pad, :] = ext_ref[ts:ts + pad, :]


def _prompt_conv(z, w_dw, b_dw, ln_g, ln_b, ts=512, rc=16):
    seq = z.shape[0]
    vec = pl.BlockSpec((1, D_CONV), lambda i: (0, 0))
    return pl.pallas_call(
        functools.partial(_pconv_kernel, ts=ts, rc=rc),
        grid=(seq // ts,),
        in_specs=[
            pl.BlockSpec((ts, D_CONV), lambda i: (i, CA_BLK)),
            pl.BlockSpec((ts, D_CONV), lambda i: (i, CB_BLK)),
            pl.BlockSpec((CONV_WIDTH, D_CONV), lambda i: (0, 0)),
            vec, vec, vec,
        ],
        out_specs=[
            pl.BlockSpec((ts, D_CONV), lambda i: (i, 0)),
            pl.BlockSpec((HALO, D_CONV), lambda i: (0, 0)),
        ],
        out_shape=[
            jax.ShapeDtypeStruct((seq, D_CONV), BF16),
            jax.ShapeDtypeStruct((HALO, D_CONV), F32),
        ],
        scratch_shapes=[pltpu.VMEM((ts + 32, D_CONV), F32),
                        pltpu.VMEM((7, ts + 24, D_CONV), F32),
                        pltpu.VMEM((CONV_WIDTH, 8, D_CONV), F32),
                        pltpu.VMEM((ts, D_CONV), F32)],
        compiler_params=_params(("arbitrary",)),
        name="prompt_conv",
    )(z, z, w_dw, b_dw, ln_g, ln_b)


def _sconv_kernel(z_ref, st_ref, wdw_ref, bdw_ref, lg_ref, lb_ref, t_ref, nb_ref, ext_ref, *, ds):
    ca = z_ref[0, :, CA_BLK * D_CONV:(CA_BLK + 1) * D_CONV]
    cb = z_ref[0, :, CB_BLK * D_CONV:(CB_BLK + 1) * D_CONV]
    ext_ref[0:HALO, :] = st_ref[0]
    ext_ref[HALO:HALO + ds, :] = ca * jax.nn.sigmoid(cb)
    acc = jnp.zeros((ds, D_CONV), F32)
    for j in range(CONV_WIDTH):
        acc = acc + wdw_ref[j:j + 1, :] * ext_ref[j:j + ds, :]
    y = acc + bdw_ref[...]
    t_ref[0] = _ln_swish(y, lg_ref[...], lb_ref[...])
    nb_ref[0] = ext_ref[ds:ds + HALO, :]


def _sample_conv(zs3, state, w_dw, b_dw, ln_g, ln_b):
    db, ds, _ = zs3.shape
    vec = pl.BlockSpec((1, D_CONV), lambda b: (0, 0))
    return pl.pallas_call(
        functools.partial(_sconv_kernel, ds=ds),
        grid=(db,),
        in_specs=[
            pl.BlockSpec((1, ds, D_IN), lambda b: (b, 0, 0)),
            pl.BlockSpec((1, HALO, D_CONV), lambda b: (b, 0, 0)),
            pl.BlockSpec((CONV_WIDTH, D_CONV), lambda b: (0, 0)),
            vec, vec, vec,
        ],
        out_specs=[
            pl.BlockSpec((1, ds, D_CONV), lambda b: (b, 0, 0)),
            pl.BlockSpec((1, HALO, D_CONV), lambda b: (b, 0, 0)),
        ],
        out_shape=[
            jax.ShapeDtypeStruct((db, ds, D_CONV), F32),
            jax.ShapeDtypeStruct((db, HALO, D_CONV), F32),
        ],
        scratch_shapes=[pltpu.VMEM((HALO + 8, D_CONV), F32)],
        compiler_params=_params(("arbitrary",)),
        name="sample_conv",
    )(zs3, state, w_dw, b_dw, ln_g, ln_b)


def _outproj_kernel(x_ref, ya_ref, t_ref, gc_ref, wpw_ref, bpw_ref, wout_ref, fg_ref, y_ref):
    yc = jnp.dot(t_ref[...].astype(BF16), wpw_ref[...], preferred_element_type=F32) + bpw_ref[...]
    yc = yc * _silu(gc_ref[...])
    y = jnp.dot(ya_ref[...].astype(BF16), wout_ref[0:D_ATTN, :], preferred_element_type=F32)
    y = y + jnp.dot(yc.astype(BF16), wout_ref[D_ATTN:D_MODEL, :], preferred_element_type=F32)
    r = x_ref[...] + y
    ms = jnp.mean(r * r, axis=-1, keepdims=True)
    y_ref[...] = r * lax.rsqrt(ms + EPS) * fg_ref[...]


def _outproj(x, y_attn, t, z, w_pw_bf16, b_pw, w_out_bf16, final_g, tm):
    m = x.shape[0]
    return pl.pallas_call(
        _outproj_kernel,
        grid=(m // tm,),
        in_specs=[
            pl.BlockSpec((tm, D_MODEL), lambda i: (i, 0)),
            pl.BlockSpec((tm, D_ATTN), lambda i: (i, 0)),
            pl.BlockSpec((tm, D_CONV), lambda i: (i, 0)),
            pl.BlockSpec((tm, D_CONV), lambda i: (i, GC_BLK)),
            pl.BlockSpec((D_CONV, D_CONV), lambda i: (0, 0)),
            pl.BlockSpec((1, D_CONV), lambda i: (0, 0)),
            pl.BlockSpec((D_MODEL, D_MODEL), lambda i: (0, 0)),
            pl.BlockSpec((1, D_MODEL), lambda i: (0, 0)),
        ],
        out_specs=pl.BlockSpec((tm, D_MODEL), lambda i: (i, 0)),
        out_shape=jax.ShapeDtypeStruct((m, D_MODEL), F32),
        compiler_params=_params(("arbitrary",)),
        name="outproj",
    )(x, y_attn, t, z, w_pw_bf16, b_pw, w_out_bf16, final_g)


def kernel(x_prompt, x_sample, cache_k, cache_v, state_conv, page_table, norm_g, w_in, lam_p, subln_g,
           w_dw, b_dw, conv_ln_g, conv_ln_b, w_pw, b_pw, w_out, rel_bias, final_g):
    depth = w_in.shape[0]
    assert depth == 1, "single-layer step"
    bsz, seq, _ = x_prompt.shape
    assert bsz == 1
    db, ds, _ = x_sample.shape
    lam_init = 0.8 - 0.6 * math.exp(-0.3 * 0)
    t_attn = 512

    w_in_l = w_in[0].astype(BF16)
    w_pw_b = w_pw[0].astype(BF16)
    w_out_b = w_out[0].astype(BF16)
    g = norm_g[0][None, :]
    sg = subln_g[0][None, :]
    bdw, lg, lb, bpw = b_dw[0][None, :], conv_ln_g[0][None, :], conv_ln_b[0][None, :], b_pw[0][None, :]
    fg = final_g[None, :]

    xp = x_prompt.reshape(seq, D_MODEL)
    xs = x_sample.reshape(db * ds, D_MODEL)

    zp = _inproj(xp, g, w_in_l, tm=1024, tn=512)
    bsub, bdiag = _prompt_bias(rel_bias, t_attn)
    ya_p = _prompt_attention(zp, bsub, bdiag, lam_p[0], sg, t_attn, lam_init)
    t_p, nc_p = _prompt_conv(zp, w_dw[0], bdw, lg, lb)
    y_p = _outproj(xp, ya_p, t_p, zp, w_pw_b, bpw, w_out_b, fg, tm=256)

    zs = _inproj(xs, g, w_in_l, tm=db * ds, tn=512)
    bpast, bnew = _sample_bias(rel_bias, ds)
    ya_s = _sample_attention(zs, cache_k[0], cache_v[0], page_table, bpast, bnew, lam_p[0], sg, ds, lam_init)
    zs3 = zs.reshape(db, ds, D_IN)
    t_s, nc_s = _sample_conv(zs3, state_conv[0], w_dw[0], bdw, lg, lb)
    y_s = _outproj(xs, ya_s.reshape(db * ds, D_ATTN), t_s.reshape(db * ds, D_CONV), zs,
                   w_pw_b, bpw, w_out_b, fg, tm=db * ds)

    return (
        y_p.reshape(1, seq, D_MODEL),
        y_s.reshape(db, ds, D_MODEL),
        zp[:, D_ATTN:2 * D_ATTN].reshape(1, 1, seq, N_HEADS, 2 * DK),
        zp[:, 2 * D_ATTN:3 * D_ATTN].reshape(1, 1, seq, N_HEADS, DV),
        nc_p.reshape(1, 1, HALO, D_CONV),
        zs[:, D_ATTN:2 * D_ATTN].reshape(1, db, ds, N_HEADS, 2 * DK),
        zs[:, 2 * D_ATTN:3 * D_ATTN].reshape(1, db, ds, N_HEADS, DV),
        nc_s.reshape(1, db, HALO, D_CONV),
    )
```

```python
import functools
import math

import jax
import jax.numpy as jnp
import numpy as np
from jax import lax
from jax.experimental import pallas as pl
from jax.experimental.pallas import tpu as pltpu

F32 = jnp.float32
BF16 = jnp.bfloat16

N_HEADS = 8
DV = 128
DK = 64
D_ATTN = N_HEADS * DV
D_CONV = 1024
D_MODEL = D_ATTN + D_CONV
D_IN = 4 * D_ATTN + 3 * D_CONV
CONV_WIDTH = 31
HALO = CONV_WIDTH - 1
N_BUCKETS = 32
MAX_DISTANCE = 128
PAGE_SIZE = 128
EPS = 1e-6
NEG = -1e30
LOG2E = math.log2(math.e)

QB, KB, VB, GAB = 0, N_HEADS, 2 * N_HEADS, 3 * N_HEADS
CA_BLK, CB_BLK, GC_BLK = 4, 5, 6

VMEM_LIMIT = 56 * 1024 * 1024
ATTN_VMEM_LIMIT = 60 * 1024 * 1024


def _params(sem, vmem=VMEM_LIMIT):
    return pltpu.CompilerParams(dimension_semantics=sem, vmem_limit_bytes=vmem)


def _bucket_thresholds():
    n = np.arange(0, MAX_DISTANCE + 1)
    max_exact = N_BUCKETS // 2
    nf = np.maximum(n, 1).astype(np.float32)
    large = max_exact + (np.log(nf / np.float32(max_exact)) / np.float32(math.log(MAX_DISTANCE / max_exact))
                         * np.float32(N_BUCKETS - max_exact)).astype(np.int32)
    large = np.minimum(large, N_BUCKETS - 1)
    bucket = np.where(n < max_exact, n, large)
    thr = []
    for b in range(1, N_BUCKETS):
        hit = np.nonzero(bucket >= b)[0]
        thr.append(int(hit[0]))
    assert bucket[-1] == N_BUCKETS - 1 and all(np.diff(bucket) >= 0)
    return tuple(thr)


_THR = _bucket_thresholds()


def _silu(x):
    return x * jax.nn.sigmoid(x)


def _inproj_kernel(x_ref, g_ref, w_ref, z_ref, h_ref):
    @pl.when(pl.program_id(1) == 0)
    def _():
        x = x_ref[...]
        ms = jnp.mean(x * x, axis=-1, keepdims=True)
        h_ref[...] = (x * lax.rsqrt(ms + EPS) * g_ref[...]).astype(BF16)

    z_ref[...] = jnp.dot(h_ref[...], w_ref[...], preferred_element_type=F32)


def _inproj(x, g, w, tm, tn):
    m = x.shape[0]
    return pl.pallas_call(
        _inproj_kernel,
        grid=(m // tm, D_IN // tn),
        in_specs=[
            pl.BlockSpec((tm, D_MODEL), lambda i, j: (i, 0)),
            pl.BlockSpec((1, D_MODEL), lambda i, j: (0, 0)),
            pl.BlockSpec((D_MODEL, tn), lambda i, j: (0, j)),
        ],
        out_specs=pl.BlockSpec((tm, tn), lambda i, j: (i, j)),
        out_shape=jax.ShapeDtypeStruct((m, D_IN), F32),
        scratch_shapes=[pltpu.VMEM((tm, D_MODEL), BF16)],
        compiler_params=_params(("arbitrary", "arbitrary")),
        name="inproj",
    )(x, g, w)


def _rel_bias_value(rb_ref, h, dist):
    val = jnp.full(dist.shape, rb_ref[0, h], F32)
    for b in range(1, N_BUCKETS):
        val = jnp.where(dist >= _THR[b - 1], rb_ref[b, h], val)
    val = val - rb_ref[N_BUCKETS - 1, h]
    return jnp.where(dist < 0, NEG, val)


def _bias_kernel(rb_ref, sub_ref, diag_ref, *, t):
    h = pl.program_id(0)
    r = lax.broadcasted_iota(jnp.int32, (t, t), 0)
    c = lax.broadcasted_iota(jnp.int32, (t, t), 1)
    diag_ref[0] = _rel_bias_value(rb_ref, h, r - c) * LOG2E
    sub_ref[0] = _rel_bias_value(rb_ref, h, r - c + t) * LOG2E


def _prompt_bias(rel_bias, t):
    shp = jax.ShapeDtypeStruct((N_HEADS, t, t), F32)
    return pl.pallas_call(
        functools.partial(_bias_kernel, t=t),
        grid=(N_HEADS,),
        in_specs=[pl.BlockSpec(memory_space=pltpu.SMEM)],
        out_specs=[pl.BlockSpec((1, t, t), lambda h: (h, 0, 0))] * 2,
        out_shape=[shp, shp],
        compiler_params=_params(("arbitrary",)),
        name="prompt_bias",
    )(rel_bias)


def _sample_bias_kernel(rb_ref, past_ref, new_ref, *, ds):
    h = pl.program_id(0)
    npast = PAGE_SIZE * N_HEADS
    row = lax.broadcasted_iota(jnp.int32, (8, npast), 0)
    col = lax.broadcasted_iota(jnp.int32, (8, npast), 1)
    past_ref[...] = _rel_bias_value(rb_ref, h, row % ds + PAGE_SIZE - col // N_HEADS)
    row = lax.broadcasted_iota(jnp.int32, (8, 128), 0)
    col = lax.broadcasted_iota(jnp.int32, (8, 128), 1)
    new = _rel_bias_value(rb_ref, h, row % ds - col // N_HEADS)
    own = jnp.logical_and(col < ds * N_HEADS, col % N_HEADS == h)
    new_ref[...] = jnp.where(own, new, NEG)


def _sample_bias(rel_bias, ds):
    npast = PAGE_SIZE * N_HEADS
    return pl.pallas_call(
        functools.partial(_sample_bias_kernel, ds=ds),
        grid=(N_HEADS,),
        in_specs=[pl.BlockSpec(memory_space=pltpu.SMEM)],
        out_specs=[pl.BlockSpec((8, npast), lambda h: (h, 0)),
                   pl.BlockSpec((8, 128), lambda h: (h, 0))],
        out_shape=[jax.ShapeDtypeStruct((N_HEADS * 8, npast), F32),
                   jax.ShapeDtypeStruct((N_HEADS * 8, 128), F32)],
        compiler_params=_params(("arbitrary",)),
        name="sample_bias",
    )(rel_bias)


def _lambda(lamp_ref, lam_init):
    lp = lamp_ref[...]
    a = jnp.sum(lp[0:1] * lp[1:2], axis=1, keepdims=True)
    b = jnp.sum(lp[2:3] * lp[3:4], axis=1, keepdims=True)
    return jnp.exp(a) - jnp.exp(b) + lam_init


def _head_epilogue(o0, o1, lam, sg, ga, lam_init):
    o = o0 - lam * o1
    ms = jnp.mean(o * o, axis=-1, keepdims=True)
    o = o * lax.rsqrt(ms + EPS) * sg * (1.0 - lam_init)
    return o * _silu(ga)


def _pattn_kernel(q_ref, k_ref, v_ref, ga_ref, bsub_ref, bdiag_ref, lamp_ref, sg_ref,
                  o_ref, kt_ref, vx_ref, m_ref, acc_ref, *, t, seq, group, lam_init):
    i = pl.program_id(1)
    prep = 512

    @pl.when(i == 0)
    def _prep():
        def body(c, carry):
            r0 = pl.multiple_of(c * prep, prep)
            kt_ref[:, pl.ds(r0, prep)] = k_ref[pl.ds(r0, prep), :].T.astype(BF16)
            vx_ref[pl.ds(r0, prep), 0:DV] = v_ref[pl.ds(r0, prep), :].astype(BF16)
            return carry
        lax.fori_loop(0, seq // prep, body, 0)
        vx_ref[:, DV:2 * DV] = jnp.ones((seq, DV), BF16)

    q = q_ref[...] * (DK ** -0.5 * LOG2E)
    lane = lax.broadcasted_iota(jnp.int32, (t, 2 * DK), 1)
    qa = jnp.where(lane < DK, q, 0.0).astype(BF16)
    qb = jnp.where(lane >= DK, q, 0.0).astype(BF16)
    qs = jnp.concatenate([qa, qb], axis=0)

    m_ref[...] = jnp.full(m_ref.shape, NEG, F32)
    acc_ref[...] = jnp.zeros(acc_ref.shape, F32)

    def scores(blk, bias):
        c0 = pl.multiple_of(blk * t, t)
        s = jnp.dot(qs, kt_ref[:, pl.ds(c0, t)], preferred_element_type=F32)
        if bias is not None:
            s = (s.reshape(2, t, t) + bias[None]).reshape(2 * t, t)
        return s

    def update(blk, s):
        c0 = pl.multiple_of(blk * t, t)
        m_prev = m_ref[...]
        m_next = jnp.maximum(m_prev, jnp.max(s, axis=1, keepdims=True))
        alpha = jnp.exp2(m_prev - m_next)
        p = jnp.exp2(s - jnp.tile(m_next, (1, t // 128)))
        pv = jnp.dot(p.astype(BF16), vx_ref[pl.ds(c0, t), :], preferred_element_type=F32)
        acc_ref[...] = jnp.tile(alpha, (1, 2)) * acc_ref[...] + pv
        m_ref[...] = m_next

    def steps(blocks):
        ss = [scores(blk, bias) for blk, bias in blocks]
        for (blk, _), s in zip(blocks, ss):
            update(blk, s)

    n_full = jnp.maximum(i - 1, 0)

    def group_body(jj, carry):
        steps([(group * jj + g, None) for g in range(group)])
        return carry
    lax.fori_loop(0, n_full // group, group_body, 0)

    @pl.when(i == 0)
    def _first_tile():
        steps([(0, bdiag_ref[0])])

    for rem in range(group):
        @pl.when(jnp.logical_and(i >= 1, n_full % group == rem))
        def _tail():
            left = [(i - 1 - rem + g, None) for g in range(rem)]
            steps(left + [(i - 1, bsub_ref[0]), (i, bdiag_ref[0])])

    acc = acc_ref[...]
    o_all = acc[:, 0:DV] / acc[:, DV:2 * DV]
    lam = _lambda(lamp_ref, lam_init)
    o_ref[...] = _head_epilogue(o_all[0:t], o_all[t:2 * t], lam, sg_ref[...], ga_ref[...], lam_init)


def _prompt_attention(z, bsub, bdiag, lam_p, subln_g, t, lam_init, group=4):
    seq = z.shape[0]
    kern = functools.partial(_pattn_kernel, t=t, seq=seq, group=group, lam_init=lam_init)
    return pl.pallas_call(
        kern,
        grid=(N_HEADS, seq // t),
        in_specs=[
            pl.BlockSpec((t, DV), lambda h, i: (i, QB + h)),
            pl.BlockSpec((seq, DV), lambda h, i: (0, KB + h)),
            pl.BlockSpec((seq, DV), lambda h, i: (0, VB + h)),
            pl.BlockSpec((t, DV), lambda h, i: (i, GAB + h)),
            pl.BlockSpec((1, t, t), lambda h, i: (h, 0, 0)),
            pl.BlockSpec((1, t, t), lambda h, i: (h, 0, 0)),
            pl.BlockSpec((4, DK), lambda h, i: (0, 0)),
            pl.BlockSpec((1, DV), lambda h, i: (0, 0)),
        ],
        out_specs=pl.BlockSpec((t, DV), lambda h, i: (i, h)),
        out_shape=jax.ShapeDtypeStruct((seq, D_ATTN), F32),
        scratch_shapes=[
            pltpu.VMEM((DV, seq), BF16),
            pltpu.VMEM((seq, 2 * DV), BF16),
            pltpu.VMEM((2 * t, 128), F32),
            pltpu.VMEM((2 * t, 2 * DV), F32),
        ],
        compiler_params=_params(("arbitrary", "arbitrary")),
        name="prompt_attention",
    )(z, z, z, z, bsub, bdiag, lam_p, subln_g)


def _sattn_kernel(pt_ref, zq_ref, kn_ref, vn_ref, ck_ref, cv_ref, bpast_ref, bnew_ref, lamp_ref, sg_ref,
                  o_ref, kbuf, vbuf, sem, *, ds, n_pages, ppc, sub, lam_init):
    b = pl.program_id(0)
    nb = pl.num_programs(0)
    n_chunks = n_pages // ppc
    rows = PAGE_SIZE * N_HEADS
    nq = N_HEADS * 2 * ds

    def page_copies(req, chunk, slot):
        cps = []
        for pg in range(ppc):
            page = pt_ref[req, chunk * ppc + pg]
            cps.append(pltpu.make_async_copy(
                ck_ref.at[page], kbuf.at[slot, pl.ds(pg * rows, rows), :], sem.at[0, slot]))
            cps.append(pltpu.make_async_copy(
                cv_ref.at[page], vbuf.at[slot, pl.ds(pg * rows, rows), :], sem.at[1, slot]))
        return cps

    def start_chunk(req, chunk, slot):
        for cp in page_copies(req, chunk, slot):
            cp.start()

    def wait_chunk(req, chunk, slot):
        for cp in page_copies(req, chunk, slot):
            cp.wait()

    n_slots = kbuf.shape[0]
    ahead = n_slots - 1

    @pl.when(b == 0)
    def _prime():
        for g in range(ahead):
            start_chunk(g // n_chunks, g % n_chunks, g % n_slots)

    zq = zq_ref[0]
    lam = _lambda(lamp_ref, lam_init)
    row = lax.broadcasted_iota(jnp.int32, (2 * ds, 2 * DK), 0)
    lane = lax.broadcasted_iota(jnp.int32, (2 * ds, 2 * DK), 1)
    own_map = (lane >= DK) == (row >= ds)

    qrows = []
    for h in range(N_HEADS):
        qf = zq[:, (QB + h) * 128:(QB + h + 1) * 128] * (DK ** -0.5)
        qrows.append(jnp.where(own_map, jnp.concatenate([qf, qf], axis=0), 0.0))
    qall = jnp.concatenate(qrows, axis=0).astype(BF16)

    mrow = lax.broadcasted_iota(jnp.int32, (nq, 128), 0)
    mcol = lax.broadcasted_iota(jnp.int32, (nq, 128), 1)
    head_mask = jnp.where(mcol % N_HEADS == mrow // (2 * ds), 0.0, NEG)

    def block_softmax(s, vf):
        m = jnp.max(s, axis=1, keepdims=True)
        p = jnp.exp(s - m)
        l = jnp.sum(p, axis=1, keepdims=True)
        return m, l, jnp.dot(p.astype(BF16), vf, preferred_element_type=F32)

    def merge(parts):
        m = functools.reduce(jnp.maximum, [pm for pm, _, _ in parts])
        scale = [jnp.exp(pm - m) for pm, _, _ in parts]
        l = sum(sc * pl_ for sc, (_, pl_, _) in zip(scale, parts))
        acc = sum(sc * pa for sc, (_, _, pa) in zip(scale, parts))
        return m, l, acc

    def scores(kf):
        return lax.dot_general(qall, kf, (((1,), (1,)), ((), ())), preferred_element_type=F32)

    def chunk_body(c, carry):
        g = b * n_chunks + c
        slot = g % n_slots
        nxt = c + ahead
        nxt_req = b + nxt // n_chunks

        @pl.when(nxt_req < nb)
        def _():
            start_chunk(nxt_req, nxt % n_chunks, (g + ahead) % n_slots)

        wait_chunk(b, c, slot)
        n_sub = ppc * rows // sub
        ss = []
        for sb in range(n_sub):
            kf = kbuf[slot, sb * sub:(sb + 1) * sub, :].astype(BF16)
            s = scores(kf) + jnp.tile(head_mask, (1, sub // 128))
            if sb == n_sub - 1:
                tail = jnp.where(c == n_chunks - 1, bpast_ref[...], 0.0)
                s = jnp.concatenate([s[:, :sub - rows], s[:, sub - rows:] + tail], axis=1)
            ss.append(s)
        parts = [carry]
        for sb in range(n_sub):
            vf = vbuf[slot, sb * sub:(sb + 1) * sub, :].astype(BF16)
            parts.append(block_softmax(ss[sb], vf))
        return merge(parts)

    init = (jnp.full((nq, 1), NEG, F32), jnp.zeros((nq, 1), F32), jnp.zeros((nq, DV), F32))
    carry = lax.fori_loop(0, n_chunks, chunk_body, init)

    pad = jnp.zeros((128 - ds * N_HEADS, DV), F32)
    kn = jnp.concatenate([kn_ref[0], pad], axis=0).astype(BF16)
    vn = jnp.concatenate([vn_ref[0], pad], axis=0).astype(BF16)
    _, l, acc = merge([carry, block_softmax(scores(kn) + bnew_ref[...], vn)])
    o_all = acc / l
    for h in range(N_HEADS):
        r0 = h * 2 * ds
        ga = zq[:, (GAB + h) * 128:(GAB + h + 1) * 128]
        o_ref[0, :, h * DV:(h + 1) * DV] = _head_epilogue(
            o_all[r0:r0 + ds], o_all[r0 + ds:r0 + 2 * ds], lam, sg_ref[...], ga, lam_init)


def _sample_attention(zs, cache_k, cache_v, page_table, bpast, bnew, lam_p, subln_g, ds, lam_init,
                      ppc=8, sub=2048, n_slots=3):
    db, n_pages = page_table.shape
    rows = PAGE_SIZE * N_HEADS
    n_pool = cache_k.shape[0]
    nq = N_HEADS * 2 * ds
    assert 2 * ds == 8 and n_pages % ppc == 0 and (ppc * rows) % sub == 0 and sub >= rows
    ck = cache_k.reshape(n_pool, rows, 2 * DK)
    cv = cache_v.reshape(n_pool, rows, DV)
    kn = zs[:, D_ATTN:2 * D_ATTN].reshape(db, ds * N_HEADS, 2 * DK)
    vn = zs[:, 2 * D_ATTN:3 * D_ATTN].reshape(db, ds * N_HEADS, DV)
    kern = functools.partial(_sattn_kernel, ds=ds, n_pages=n_pages, ppc=ppc, sub=sub, lam_init=lam_init)
    grid_spec = pltpu.PrefetchScalarGridSpec(
        num_scalar_prefetch=1,
        grid=(db,),
        in_specs=[
            pl.BlockSpec((1, ds, D_IN), lambda b, pt: (b, 0, 0)),
            pl.BlockSpec((1, ds * N_HEADS, 2 * DK), lambda b, pt: (b, 0, 0)),
            pl.BlockSpec((1, ds * N_HEADS, DV), lambda b, pt: (b, 0, 0)),
            pl.BlockSpec(memory_space=pl.ANY),
            pl.BlockSpec(memory_space=pl.ANY),
            pl.BlockSpec((nq, rows), lambda b, pt: (0, 0)),
            pl.BlockSpec((nq, 128), lambda b, pt: (0, 0)),
            pl.BlockSpec((4, DK), lambda b, pt: (0, 0)),
            pl.BlockSpec((1, DV), lambda b, pt: (0, 0)),
        ],
        out_specs=pl.BlockSpec((1, ds, D_ATTN), lambda b, pt: (b, 0, 0)),
        scratch_shapes=[
            pltpu.VMEM((n_slots, ppc * rows, 2 * DK), F32),
            pltpu.VMEM((n_slots, ppc * rows, DV), F32),
            pltpu.SemaphoreType.DMA((2, n_slots)),
        ],
    )
    return pl.pallas_call(
        kern,
        grid_spec=grid_spec,
        out_shape=jax.ShapeDtypeStruct((db, ds, D_ATTN), F32),
        compiler_params=_params(("arbitrary",)),
        name="sample_attention",
    )(page_table, zs.reshape(db, ds, D_IN), kn, vn, ck, cv, bpast, bnew, lam_p, subln_g)


def _attn_kernel(pt_ref,
                 q_ref, k_ref, v_ref, ga_ref, bsub_ref, bdiag_ref, lamp_ref, sg_ref,
                 sq_ref, sga_ref, kn_ref, vn_ref, ck_ref, cv_ref, bpast_ref, bnew_ref,
                 o_ref, so_ref,
                 kt_ref, vx_ref, m_ref, acc_ref,
                 kbuf, vbuf, sem, qall_ref, sm_ref, sl_ref, sacc_ref, count_ref,
                 *, t, seq, group, ds, n_req, n_pages, ppc, sub, chunks_per_slot, lam_init):
    hd = pl.program_id(0)
    i = pl.program_id(1)
    first_step = jnp.logical_and(hd == 0, i == 0)
    last_step = jnp.logical_and(hd == pl.num_programs(0) - 1, i == pl.num_programs(1) - 1)

    n_chunks = n_pages // ppc
    total = n_req * n_chunks
    rows = PAGE_SIZE * N_HEADS
    nq = N_HEADS * 2 * ds
    n_slots = kbuf.shape[0]
    ahead = n_slots - 1
    n_sub = ppc * rows // sub

    def page_copies(g):
        req, chunk, slot = g // n_chunks, g % n_chunks, g % n_slots
        cps = []
        for pg in range(ppc):
            page = pt_ref[req, chunk * ppc + pg]
            cps.append(pltpu.make_async_copy(
                ck_ref.at[page], kbuf.at[slot, pl.ds(pg * rows, rows), :], sem.at[0, slot]))
            cps.append(pltpu.make_async_copy(
                cv_ref.at[page], vbuf.at[slot, pl.ds(pg * rows, rows), :], sem.at[1, slot]))
        return cps

    @pl.when(first_step)
    def _prime():
        count_ref[0] = 0
        for g in range(ahead):
            for cp in page_copies(g):
                cp.start()

    def block_softmax(s, vf):
        m = jnp.max(s, axis=1, keepdims=True)
        p = jnp.exp(s - m)
        l = jnp.sum(p, axis=1, keepdims=True)
        return m, l, jnp.dot(p.astype(BF16), vf, preferred_element_type=F32)

    def merge(parts):
        m = functools.reduce(jnp.maximum, [pm for pm, _, _ in parts])
        scale = [jnp.exp(pm - m) for pm, _, _ in parts]
        l = sum(sc * pl_ for sc, (_, pl_, _) in zip(scale, parts))
        acc = sum(sc * pa for sc, (_, _, pa) in zip(scale, parts))
        return m, l, acc

    def sample_scores(kf):
        return lax.dot_general(qall_ref[...], kf, (((1,), (1,)), ((), ())), preferred_element_type=F32)

    def sample_chunk(g):
        req, c, slot = g // n_chunks, g % n_chunks, g % n_slots

        @pl.when(g + ahead < total)
        def _():
            for cp in page_copies(g + ahead):
                cp.start()

        for cp in page_copies(g):
            cp.wait()

        @pl.when(c == 0)
        def _new_request():
            zq = sq_ref[req]
            row = lax.broadcasted_iota(jnp.int32, (2 * ds, 2 * DK), 0)
            lane = lax.broadcasted_iota(jnp.int32, (2 * ds, 2 * DK), 1)
            own_map = (lane >= DK) == (row >= ds)
            qrows = []
            for h in range(N_HEADS):
                qf = zq[:, h * 128:(h + 1) * 128] * (DK ** -0.5)
                qrows.append(jnp.where(own_map, jnp.concatenate([qf, qf], axis=0), 0.0))
            qall_ref[...] = jnp.concatenate(qrows, axis=0).astype(BF16)
            sm_ref[...] = jnp.full(sm_ref.shape, NEG, F32)
            sl_ref[...] = jnp.zeros(sl_ref.shape, F32)
            sacc_ref[...] = jnp.zeros(sacc_ref.shape, F32)

        mrow = lax.broadcasted_iota(jnp.int32, (nq, 128), 0)
        mcol = lax.broadcasted_iota(jnp.int32, (nq, 128), 1)
        head_mask = jnp.where(mcol % N_HEADS == mrow // (2 * ds), 0.0, NEG)

        ss = []
        for sb in range(n_sub):
            kf = kbuf[slot, sb * sub:(sb + 1) * sub, :].astype(BF16)
            s = sample_scores(kf) + jnp.tile(head_mask, (1, sub // 128))
            if sb == n_sub - 1:
                tail = jnp.where(c == n_chunks - 1, bpast_ref[...], 0.0)
                s = jnp.concatenate([s[:, :sub - rows], s[:, sub - rows:] + tail], axis=1)
            ss.append(s)
        parts = [(sm_ref[...], sl_ref[...], sacc_ref[...])]
        for sb in range(n_sub):
            vf = vbuf[slot, sb * sub:(sb + 1) * sub, :].astype(BF16)
            parts.append(block_softmax(ss[sb], vf))
        m, l, acc = merge(parts)
        sm_ref[...] = m
        sl_ref[...] = l
        sacc_ref[...] = acc

        @pl.when(c == n_chunks - 1)
        def _finish_request():
            pad = jnp.zeros((128 - ds * N_HEADS, DV), F32)
            kn = jnp.concatenate([kn_ref[req], pad], axis=0).astype(BF16)
            vn = jnp.concatenate([vn_ref[req], pad], axis=0).astype(BF16)
            _, lf, accf = merge([(sm_ref[...], sl_ref[...], sacc_ref[...]),
                                 block_softmax(sample_scores(kn) + bnew_ref[...], vn)])
            o_all = accf / lf
            lam = _lambda(lamp_ref, lam_init)
            ga = sga_ref[req]
            for h in range(N_HEADS):
                r0 = h * 2 * ds
                so_ref[req, :, h * DV:(h + 1) * DV] = _head_epilogue(
                    o_all[r0:r0 + ds], o_all[r0 + ds:r0 + 2 * ds], lam, sg_ref[...],
                    ga[:, h * DV:(h + 1) * DV], lam_init)

    def sample_work(n):
        def body(_, carry):
            g = count_ref[0]

            @pl.when(g < total)
            def _():
                sample_chunk(g)
                count_ref[0] = g + 1
            return carry
        lax.fori_loop(0, n, body, 0)

    prep = 512

    @pl.when(i == 0)
    def _prep():
        def body(c, carry):
            r0 = pl.multiple_of(c * prep, prep)
            kt_ref[:, pl.ds(r0, prep)] = k_ref[pl.ds(r0, prep), :].T.astype(BF16)
            vx_ref[pl.ds(r0, prep), 0:DV] = v_ref[pl.ds(r0, prep), :].astype(BF16)
            return carry
        lax.fori_loop(0, seq // prep, body, 0)
        vx_ref[:, DV:2 * DV] = jnp.ones((seq, DV), BF16)

    q = q_ref[...] * (DK ** -0.5 * LOG2E)
    lane = lax.broadcasted_iota(jnp.int32, (t, 2 * DK), 1)
    qa = jnp.where(lane < DK, q, 0.0).astype(BF16)
    qb = jnp.where(lane >= DK, q, 0.0).astype(BF16)
    qs = jnp.concatenate([qa, qb], axis=0)

    m_ref[...] = jnp.full(m_ref.shape, NEG, F32)
    acc_ref[...] = jnp.zeros(acc_ref.shape, F32)

    def scores(blk, bias):
        c0 = pl.multiple_of(blk * t, t)
        s = jnp.dot(qs, kt_ref[:, pl.ds(c0, t)], preferred_element_type=F32)
        if bias is not None:
            s = (s.reshape(2, t, t) + bias[None]).reshape(2 * t, t)
        return s

    def update(blk, s):
        c0 = pl.multiple_of(blk * t, t)
        m_prev = m_ref[...]
        m_next = jnp.maximum(m_prev, jnp.max(s, axis=1, keepdims=True))
        alpha = jnp.exp2(m_prev - m_next)
        p = jnp.exp2(s - jnp.tile(m_next, (1, t // 128)))
        pv = jnp.dot(p.astype(BF16), vx_ref[pl.ds(c0, t), :], preferred_element_type=F32)
        acc_ref[...] = jnp.tile(alpha, (1, 2)) * acc_ref[...] + pv
        m_ref[...] = m_next

    def steps(blocks):
        ss = [scores(blk, bias) for blk, bias in blocks]
        for (blk, _), s in zip(blocks, ss):
            update(blk, s)
        sample_work(chunks_per_slot)

    n_full = jnp.maximum(i - 1, 0)

    def group_body(jj, carry):
        steps([(group * jj + g, None) for g in range(group)])
        return carry
    lax.fori_loop(0, n_full // group, group_body, 0)

    @pl.when(i == 0)
    def _first_tile():
        steps([(0, bdiag_ref[0])])

    for rem in range(group):
        @pl.when(jnp.logical_and(i >= 1, n_full % group == rem))
        def _tail():
            left = [(i - 1 - rem + g, None) for g in range(rem)]
            steps(left + [(i - 1, bsub_ref[0]), (i, bdiag_ref[0])])

    acc = acc_ref[...]
    o_all = acc[:, 0:DV] / acc[:, DV:2 * DV]
    lam = _lambda(lamp_ref, lam_init)
    o_ref[...] = _head_epilogue(o_all[0:t], o_all[t:2 * t], lam, sg_ref[...], ga_ref[...], lam_init)

    @pl.when(last_step)
    def _drain():
        sample_work(total - count_ref[0])


def _attention(z, zs, cache_k, cache_v, page_table, bsub, bdiag, bpast, bnew, lam_p, subln_g, ds, lam_init,
               t=512, group=4, ppc=8, sub=2048, n_slots=3, chunks_per_slot=2):
    seq = z.shape[0]
    n_req, n_pages = page_table.shape
    rows = PAGE_SIZE * N_HEADS
    n_pool = cache_k.shape[0]
    nq = N_HEADS * 2 * ds
    assert 2 * ds == 8 and n_pages % ppc == 0 and (ppc * rows) % sub == 0 and sub >= rows
    ck = cache_k.reshape(n_pool, rows, 2 * DK)
    cv = cache_v.reshape(n_pool, rows, DV)
    zs3 = zs.reshape(n_req, ds, D_IN)
    kn = zs[:, D_ATTN:2 * D_ATTN].reshape(n_req, ds * N_HEADS, 2 * DK)
    vn = zs[:, 2 * D_ATTN:3 * D_ATTN].reshape(n_req, ds * N_HEADS, DV)
    kern = functools.partial(_attn_kernel, t=t, seq=seq, group=group, ds=ds, n_req=n_req, n_pages=n_pages,
                             ppc=ppc, sub=sub, chunks_per_slot=chunks_per_slot, lam_init=lam_init)
    once = pl.Buffered(1)
    grid_spec = pltpu.PrefetchScalarGridSpec(
        num_scalar_prefetch=1,
        grid=(N_HEADS, seq // t),
        in_specs=[
            pl.BlockSpec((t, DV), lambda h, i, pt: (i, QB + h)),
            pl.BlockSpec((seq, DV), lambda h, i, pt: (0, KB + h), pipeline_mode=once),
            pl.BlockSpec((seq, DV), lambda h, i, pt: (0, VB + h), pipeline_mode=once),
            pl.BlockSpec((t, DV), lambda h, i, pt: (i, GAB + h)),
            pl.BlockSpec((1, t, t), lambda h, i, pt: (h, 0, 0), pipeline_mode=once),
            pl.BlockSpec((1, t, t), lambda h, i, pt: (h, 0, 0), pipeline_mode=once),
            pl.BlockSpec((4, DK), lambda h, i, pt: (0, 0)),
            pl.BlockSpec((1, DV), lambda h, i, pt: (0, 0)),
            pl.BlockSpec((n_req, ds, D_ATTN), lambda h, i, pt: (0, 0, QB // N_HEADS)),
            pl.BlockSpec((n_req, ds, D_ATTN), lambda h, i, pt: (0, 0, GAB // N_HEADS)),
            pl.BlockSpec((n_req, ds * N_HEADS, 2 * DK), lambda h, i, pt: (0, 0, 0)),
            pl.BlockSpec((n_req, ds * N_HEADS, DV), lambda h, i, pt: (0, 0, 0)),
            pl.BlockSpec(memory_space=pl.ANY),
            pl.BlockSpec(memory_space=pl.ANY),
            pl.BlockSpec((nq, rows), lambda h, i, pt: (0, 0)),
            pl.BlockSpec((nq, 128), lambda h, i, pt: (0, 0)),
        ],
        out_specs=[
            pl.BlockSpec((t, DV), lambda h, i, pt: (i, h)),
            pl.BlockSpec((n_req, ds, D_ATTN), lambda h, i, pt: (0, 0, 0)),
        ],
        scratch_shapes=[
            pltpu.VMEM((DV, seq), BF16),
            pltpu.VMEM((seq, 2 * DV), BF16),
            pltpu.VMEM((2 * t, 128), F32),
            pltpu.VMEM((2 * t, 2 * DV), F32),
            pltpu.VMEM((n_slots, ppc * rows, 2 * DK), F32),
            pltpu.VMEM((n_slots, ppc * rows, DV), F32),
            pltpu.SemaphoreType.DMA((2, n_slots)),
            pltpu.VMEM((nq, 2 * DK), BF16),
            pltpu.VMEM((nq, 1), F32),
            pltpu.VMEM((nq, 1), F32),
            pltpu.VMEM((nq, DV), F32),
            pltpu.SMEM((1,), jnp.int32),
        ],
    )
    return pl.pallas_call(
        kern,
        grid_spec=grid_spec,
        out_shape=[jax.ShapeDtypeStruct((seq, D_ATTN), F32),
                   jax.ShapeDtypeStruct((n_req, ds, D_ATTN), F32)],
        compiler_params=_params(("arbitrary", "arbitrary"), vmem=ATTN_VMEM_LIMIT),
        name="attention",
    )(page_table, z, z, z, z, bsub, bdiag, lam_p, subln_g, zs3, zs3, kn, vn, ck, cv, bpast, bnew)


def _ln_swish(y, lg, lb):
    mu = jnp.mean(y, axis=-1, keepdims=True)
    yc = y - mu
    var = jnp.mean(yc * yc, axis=-1, keepdims=True)
    return _silu(yc * lax.rsqrt(var + EPS) * lg + lb)


def _pconv_kernel(ca_ref, cb_ref, wdw_ref, bdw_ref, lg_ref, lb_ref, t_ref, nb_ref, ext_ref, sh_ref, wb_ref,
                  y_ref, *, ts, rc):
    i = pl.program_id(0)
    pad = 32
    first = pad - HALO

    @pl.when(i == 0)
    def _():
        ext_ref[0:pad, :] = jnp.zeros((pad, D_CONV), F32)

    ext_ref[pad:pad + ts, :] = ca_ref[...] * jax.nn.sigmoid(cb_ref[...])
    n_sh = sh_ref.shape[1]
    for b in range(1, 8):
        sh_ref[b - 1] = ext_ref[b:b + n_sh, :]
    for j in range(CONV_WIDTH):
        wb_ref[j] = jnp.broadcast_to(wdw_ref[j:j + 1, :], (8, D_CONV))

    def chunk(r, carry):
        r0 = pl.multiple_of(r * rc, rc)
        accs = [jnp.zeros((8, D_CONV), F32) for _ in range(rc // 8)]
        for j in range(CONV_WIDTH):
            a8, b = (first + j) // 8 * 8, (first + j) % 8
            src = ext_ref if b == 0 else sh_ref.at[b - 1]
            w8 = wb_ref[j]
            for g in range(rc // 8):
                accs[g] = accs[g] + w8 * src[pl.ds(r0 + a8 + 8 * g, 8), :]
        y_ref[pl.ds(r0, rc), :] = jnp.concatenate(accs, axis=0)
        return carry

    lax.fori_loop(0, ts // rc, chunk, 0)
    y = y_ref[...] + bdw_ref[...]
    t_ref[...] = _ln_swish(y, lg_ref[...], lb_ref[...]).astype(BF16)
    nb_ref[...] = ext_ref[pad + ts - HALO:pad + ts, :]
    ext_ref[0:pad, :] = ext_ref[ts:ts + pad, :]


def _prompt_conv(z, w_dw, b_dw, ln_g, ln_b, ts=512, rc=16):
    seq = z.shape[0]
    vec = pl.BlockSpec((1, D_CONV), lambda i: (0, 0))
    return pl.pallas_call(
        functools.partial(_pconv_kernel, ts=ts, rc=rc),
        grid=(seq // ts,),
        in_specs=[
            pl.BlockSpec((ts, D_CONV), lambda i: (i, CA_BLK)),
            pl.BlockSpec((ts, D_CONV), lambda i: (i, CB_BLK)),
            pl.BlockSpec((CONV_WIDTH, D_CONV), lambda i: (0, 0)),
            vec, vec, vec,
        ],
        out_specs=[
            pl.BlockSpec((ts, D_CONV), lambda i: (i, 0)),
            pl.BlockSpec((HALO, D_CONV), lambda i: (0, 0)),
        ],
        out_shape=[
            jax.ShapeDtypeStruct((seq, D_CONV), BF16),
            jax.ShapeDtypeStruct((HALO, D_CONV), F32),
        ],
        scratch_shapes=[pltpu.VMEM((ts + 32, D_CONV), F32),
                        pltpu.VMEM((7, ts + 24, D_CONV), F32),
                        pltpu.VMEM((CONV_WIDTH, 8, D_CONV), F32),
                        pltpu.VMEM((ts, D_CONV), F32)],
        compiler_params=_params(("arbitrary",)),
        name="prompt_conv",
    )(z, z, w_dw, b_dw, ln_g, ln_b)


def _sconv_kernel(z_ref, st_ref, wdw_ref, bdw_ref, lg_ref, lb_ref, t_ref, nb_ref, ext_ref, *, ds):
    ca = z_ref[0, :, CA_BLK * D_CONV:(CA_BLK + 1) * D_CONV]
    cb = z_ref[0, :, CB_BLK * D_CONV:(CB_BLK + 1) * D_CONV]
    ext_ref[0:HALO, :] = st_ref[0]
    ext_ref[HALO:HALO + ds, :] = ca * jax.nn.sigmoid(cb)
    acc = jnp.zeros((ds, D_CONV), F32)
    for j in range(CONV_WIDTH):
        acc = acc + wdw_ref[j:j + 1, :] * ext_ref[j:j + ds, :]
    y = acc + bdw_ref[...]
    t_ref[0] = _ln_swish(y, lg_ref[...], lb_ref[...])
    nb_ref[0] = ext_ref[ds:ds + HALO, :]


def _sample_conv(zs3, state, w_dw, b_dw, ln_g, ln_b):
    db, ds, _ = zs3.shape
    vec = pl.BlockSpec((1, D_CONV), lambda b: (0, 0))
    return pl.pallas_call(
        functools.partial(_sconv_kernel, ds=ds),
        grid=(db,),
        in_specs=[
            pl.BlockSpec((1, ds, D_IN), lambda b: (b, 0, 0)),
            pl.BlockSpec((1, HALO, D_CONV), lambda b: (b, 0, 0)),
            pl.BlockSpec((CONV_WIDTH, D_CONV), lambda b: (0, 0)),
            vec, vec, vec,
        ],
        out_specs=[
            pl.BlockSpec((1, ds, D_CONV), lambda b: (b, 0, 0)),
            pl.BlockSpec((1, HALO, D_CONV), lambda b: (b, 0, 0)),
        ],
        out_shape=[
            jax.ShapeDtypeStruct((db, ds, D_CONV), F32),
            jax.ShapeDtypeStruct((db, HALO, D_CONV), F32),
        ],
        scratch_shapes=[pltpu.VMEM((HALO + 8, D_CONV), F32)],
        compiler_params=_params(("arbitrary",)),
        name="sample_conv",
    )(zs3, state, w_dw, b_dw, ln_g, ln_b)


def _outproj_kernel(x_ref, ya_ref, t_ref, gc_ref, wpw_ref, bpw_ref, wout_ref, fg_ref, y_ref):
    yc = jnp.dot(t_ref[...].astype(BF16), wpw_ref[...], preferred_element_type=F32) + bpw_ref[...]
    yc = yc * _silu(gc_ref[...])
    y = jnp.dot(ya_ref[...].astype(BF16), wout_ref[0:D_ATTN, :], preferred_element_type=F32)
    y = y + jnp.dot(yc.astype(BF16), wout_ref[D_ATTN:D_MODEL, :], preferred_element_type=F32)
    r = x_ref[...] + y
    ms = jnp.mean(r * r, axis=-1, keepdims=True)
    y_ref[...] = r * lax.rsqrt(ms + EPS) * fg_ref[...]


def _outproj(x, y_attn, t, z, w_pw_bf16, b_pw, w_out_bf16, final_g, tm):
    m = x.shape[0]
    return pl.pallas_call(
        _outproj_kernel,
        grid=(m // tm,),
        in_specs=[
            pl.BlockSpec((tm, D_MODEL), lambda i: (i, 0)),
            pl.BlockSpec((tm, D_ATTN), lambda i: (i, 0)),
            pl.BlockSpec((tm, D_CONV), lambda i: (i, 0)),
            pl.BlockSpec((tm, D_CONV), lambda i: (i, GC_BLK)),
            pl.BlockSpec((D_CONV, D_CONV), lambda i: (0, 0)),
            pl.BlockSpec((1, D_CONV), lambda i: (0, 0)),
            pl.BlockSpec((D_MODEL, D_MODEL), lambda i: (0, 0)),
            pl.BlockSpec((1, D_MODEL), lambda i: (0, 0)),
        ],
        out_specs=pl.BlockSpec((tm, D_MODEL), lambda i: (i, 0)),
        out_shape=jax.ShapeDtypeStruct((m, D_MODEL), F32),
        compiler_params=_params(("arbitrary",)),
        name="outproj",
    )(x, y_attn, t, z, w_pw_bf16, b_pw, w_out_bf16, final_g)


def kernel(x_prompt, x_sample, cache_k, cache_v, state_conv, page_table, norm_g, w_in, lam_p, subln_g,
           w_dw, b_dw, conv_ln_g, conv_ln_b, w_pw, b_pw, w_out, rel_bias, final_g):
    depth = w_in.shape[0]
    assert depth == 1, "single-layer step"
    bsz, seq, _ = x_prompt.shape
    assert bsz == 1
    db, ds, _ = x_sample.shape
    lam_init = 0.8 - 0.6 * math.exp(-0.3 * 0)
    t_attn = 512

    w_in_l = w_in[0].astype(BF16)
    w_pw_b = w_pw[0].astype(BF16)
    w_out_b = w_out[0].astype(BF16)
    g = norm_g[0][None, :]
    sg = subln_g[0][None, :]
    bdw, lg, lb, bpw = b_dw[0][None, :], conv_ln_g[0][None, :], conv_ln_b[0][None, :], b_pw[0][None, :]
    fg = final_g[None, :]

    xp = x_prompt.reshape(seq, D_MODEL)
    xs = x_sample.reshape(db * ds, D_MODEL)

    zp = _inproj(xp, g, w_in_l, tm=1024, tn=512)
    zs = _inproj(xs, g, w_in_l, tm=db * ds, tn=512)
    bsub, bdiag = _prompt_bias(rel_bias, t_attn)
    bpast, bnew = _sample_bias(rel_bias, ds)
    ya_p, ya_s = _attention(zp, zs, cache_k[0], cache_v[0], page_table, bsub, bdiag, bpast, bnew,
                            lam_p[0], sg, ds, lam_init, t=t_attn)

    t_p, nc_p = _prompt_conv(zp, w_dw[0], bdw, lg, lb)
    y_p = _outproj(xp, ya_p, t_p, zp, w_pw_b, bpw, w_out_b, fg, tm=256)

    zs3 = zs.reshape(db, ds, D_IN)
    t_s, nc_s = _sample_conv(zs3, state_conv[0], w_dw[0], bdw, lg, lb)
    y_s = _outproj(xs, ya_s.reshape(db * ds, D_ATTN), t_s.reshape(db * ds, D_CONV), zs,
                   w_pw_b, bpw, w_out_b, fg, tm=db * ds)

    return (
        y_p.reshape(1, seq, D_MODEL),
        y_s.reshape(db, ds, D_MODEL),
        zp[:, D_ATTN:2 * D_ATTN].reshape(1, 1, seq, N_HEADS, 2 * DK),
        zp[:, 2 * D_ATTN:3 * D_ATTN].reshape(1, 1, seq, N_HEADS, DV),
        nc_p.reshape(1, 1, HALO, D_CONV),
        zs[:, D_ATTN:2 * D_ATTN].reshape(1, db, ds, N_HEADS, 2 * DK),
        zs[:, 2 * D_ATTN:3 * D_ATTN].reshape(1, db, ds, N_HEADS, DV),
        nc_s.reshape(1, db, HALO, D_CONV),
    )
```

```python
import functools
import math

import jax
import jax.numpy as jnp
import numpy as np
from jax import lax
from jax.experimental import pallas as pl
from jax.experimental.pallas import tpu as pltpu

F32 = jnp.float32
BF16 = jnp.bfloat16

N_HEADS = 8
DV = 128
DK = 64
D_ATTN = N_HEADS * DV
D_CONV = 1024
D_MODEL = D_ATTN + D_CONV
D_IN = 4 * D_ATTN + 3 * D_CONV
CONV_WIDTH = 31
HALO = CONV_WIDTH - 1
N_BUCKETS = 32
MAX_DISTANCE = 128
PAGE_SIZE = 128
EPS = 1e-6
NEG = -1e30
LOG2E = math.log2(math.e)

QB, KB, VB, GAB = 0, N_HEADS, 2 * N_HEADS, 3 * N_HEADS
CA_BLK, CB_BLK, GC_BLK = 4, 5, 6

VMEM_LIMIT = 56 * 1024 * 1024
ATTN_VMEM_LIMIT = 60 * 1024 * 1024


def _params(sem, vmem=VMEM_LIMIT):
    return pltpu.CompilerParams(dimension_semantics=sem, vmem_limit_bytes=vmem)


def _bucket_thresholds():
    n = np.arange(0, MAX_DISTANCE + 1)
    max_exact = N_BUCKETS // 2
    nf = np.maximum(n, 1).astype(np.float32)
    large = max_exact + (np.log(nf / np.float32(max_exact)) / np.float32(math.log(MAX_DISTANCE / max_exact))
                         * np.float32(N_BUCKETS - max_exact)).astype(np.int32)
    large = np.minimum(large, N_BUCKETS - 1)
    bucket = np.where(n < max_exact, n, large)
    thr = []
    for b in range(1, N_BUCKETS):
        hit = np.nonzero(bucket >= b)[0]
        thr.append(int(hit[0]))
    assert bucket[-1] == N_BUCKETS - 1 and all(np.diff(bucket) >= 0)
    return tuple(thr)


_THR = _bucket_thresholds()


def _silu(x):
    return x * jax.nn.sigmoid(x)


def _inproj_kernel(x_ref, g_ref, w_ref, z_ref, h_ref):
    @pl.when(pl.program_id(1) == 0)
    def _():
        x = x_ref[...]
        ms = jnp.mean(x * x, axis=-1, keepdims=True)
        h_ref[...] = (x * lax.rsqrt(ms + EPS) * g_ref[...]).astype(BF16)

    z_ref[...] = jnp.dot(h_ref[...], w_ref[...], preferred_element_type=F32)


def _inproj(x, g, w, tm, tn):
    m = x.shape[0]
    return pl.pallas_call(
        _inproj_kernel,
        grid=(m // tm, D_IN // tn),
        in_specs=[
            pl.BlockSpec((tm, D_MODEL), lambda i, j: (i, 0)),
            pl.BlockSpec((1, D_MODEL), lambda i, j: (0, 0)),
            pl.BlockSpec((D_MODEL, tn), lambda i, j: (0, j)),
        ],
        out_specs=pl.BlockSpec((tm, tn), lambda i, j: (i, j)),
        out_shape=jax.ShapeDtypeStruct((m, D_IN), F32),
        scratch_shapes=[pltpu.VMEM((tm, D_MODEL), BF16)],
        compiler_params=_params(("arbitrary", "arbitrary")),
        name="inproj",
    )(x, g, w)


def _rel_bias_value(rb_ref, h, dist):
    val = jnp.full(dist.shape, rb_ref[0, h], F32)
    for b in range(1, N_BUCKETS):
        val = jnp.where(dist >= _THR[b - 1], rb_ref[b, h], val)
    val = val - rb_ref[N_BUCKETS - 1, h]
    return jnp.where(dist < 0, NEG, val)


def _bias_kernel(rb_ref, sub_ref, diag_ref, *, t):
    h = pl.program_id(0)
    r = lax.broadcasted_iota(jnp.int32, (t, t), 0)
    c = lax.broadcasted_iota(jnp.int32, (t, t), 1)
    diag_ref[0] = _rel_bias_value(rb_ref, h, r - c) * LOG2E
    sub_ref[0] = _rel_bias_value(rb_ref, h, r - c + t) * LOG2E


def _prompt_bias(rel_bias, t):
    shp = jax.ShapeDtypeStruct((N_HEADS, t, t), F32)
    return pl.pallas_call(
        functools.partial(_bias_kernel, t=t),
        grid=(N_HEADS,),
        in_specs=[pl.BlockSpec(memory_space=pltpu.SMEM)],
        out_specs=[pl.BlockSpec((1, t, t), lambda h: (h, 0, 0))] * 2,
        out_shape=[shp, shp],
        compiler_params=_params(("arbitrary",)),
        name="prompt_bias",
    )(rel_bias)


def _sample_bias_kernel(rb_ref, past_ref, new_ref, *, ds):
    h = pl.program_id(0)
    npast = PAGE_SIZE * N_HEADS
    row = lax.broadcasted_iota(jnp.int32, (8, npast), 0)
    col = lax.broadcasted_iota(jnp.int32, (8, npast), 1)
    past_ref[...] = _rel_bias_value(rb_ref, h, row % ds + PAGE_SIZE - col // N_HEADS)
    row = lax.broadcasted_iota(jnp.int32, (8, 128), 0)
    col = lax.broadcasted_iota(jnp.int32, (8, 128), 1)
    new = _rel_bias_value(rb_ref, h, row % ds - col // N_HEADS)
    own = jnp.logical_and(col < ds * N_HEADS, col % N_HEADS == h)
    new_ref[...] = jnp.where(own, new, NEG)


def _sample_bias(rel_bias, ds):
    npast = PAGE_SIZE * N_HEADS
    return pl.pallas_call(
        functools.partial(_sample_bias_kernel, ds=ds),
        grid=(N_HEADS,),
        in_specs=[pl.BlockSpec(memory_space=pltpu.SMEM)],
        out_specs=[pl.BlockSpec((8, npast), lambda h: (h, 0)),
                   pl.BlockSpec((8, 128), lambda h: (h, 0))],
        out_shape=[jax.ShapeDtypeStruct((N_HEADS * 8, npast), F32),
                   jax.ShapeDtypeStruct((N_HEADS * 8, 128), F32)],
        compiler_params=_params(("arbitrary",)),
        name="sample_bias",
    )(rel_bias)


def _lambda(lamp_ref, lam_init):
    lp = lamp_ref[...]
    a = jnp.sum(lp[0:1] * lp[1:2], axis=1, keepdims=True)
    b = jnp.sum(lp[2:3] * lp[3:4], axis=1, keepdims=True)
    return jnp.exp(a) - jnp.exp(b) + lam_init


def _head_epilogue(o0, o1, lam, sg, ga, lam_init):
    o = o0 - lam * o1
    ms = jnp.mean(o * o, axis=-1, keepdims=True)
    o = o * lax.rsqrt(ms + EPS) * sg * (1.0 - lam_init)
    return o * _silu(ga)


def _attn_kernel(pt_ref,
                 q_ref, k_ref, v_ref, ga_ref, bsub_ref, bdiag_ref, lamp_ref, sg_ref,
                 sq_ref, sga_ref, kn_ref, vn_ref, ck_ref, cv_ref, bpast_ref, bnew_ref,
                 o_ref, so_ref,
                 kt_ref, vx_ref, m_ref, acc_ref,
                 kbuf, vbuf, sem, qall_ref, sm_ref, sl_ref, sacc_ref, count_ref,
                 *, t, seq, group, ds, n_req, n_pages, ppc, sub, lam_init):
    hd = pl.program_id(0)
    i = pl.program_id(1)
    first_step = jnp.logical_and(hd == 0, i == 0)
    last_step = jnp.logical_and(hd == pl.num_programs(0) - 1, i == pl.num_programs(1) - 1)

    n_chunks = n_pages // ppc
    total = n_req * n_chunks
    rows = PAGE_SIZE * N_HEADS
    nq = N_HEADS * 2 * ds
    n_slots = kbuf.shape[0]
    ahead = n_slots - 1
    n_sub = ppc * rows // sub

    def page_copies(g):
        req, chunk, slot = g // n_chunks, g % n_chunks, g % n_slots
        cps = []
        for pg in range(ppc):
            page = pt_ref[req, chunk * ppc + pg]
            cps.append(pltpu.make_async_copy(
                ck_ref.at[page], kbuf.at[slot, pl.ds(pg * rows, rows), :], sem.at[0, slot]))
            cps.append(pltpu.make_async_copy(
                cv_ref.at[page], vbuf.at[slot, pl.ds(pg * rows, rows), :], sem.at[1, slot]))
        return cps

    @pl.when(first_step)
    def _prime():
        count_ref[0] = 0
        for g in range(ahead):
            for cp in page_copies(g):
                cp.start()

    def block_softmax(s, vf):
        m = jnp.max(s, axis=1, keepdims=True)
        p = jnp.exp(s - m)
        l = jnp.sum(p, axis=1, keepdims=True)
        return m, l, jnp.dot(p.astype(BF16), vf, preferred_element_type=F32)

    def merge(parts):
        m = functools.reduce(jnp.maximum, [pm for pm, _, _ in parts])
        scale = [jnp.exp(pm - m) for pm, _, _ in parts]
        l = sum(sc * pl_ for sc, (_, pl_, _) in zip(scale, parts))
        acc = sum(sc * pa for sc, (_, _, pa) in zip(scale, parts))
        return m, l, acc

    def sample_scores(kf):
        return lax.dot_general(qall_ref[...], kf, (((1,), (1,)), ((), ())), preferred_element_type=F32)

    def chunk_begin():
        g = count_ref[0]
        valid = g < total

        @pl.when(valid)
        def _():
            @pl.when(g + ahead < total)
            def _():
                for cp in page_copies(g + ahead):
                    cp.start()

            for cp in page_copies(g):
                cp.wait()

            @pl.when(g % n_chunks == 0)
            def _new_request():
                zq = sq_ref[g // n_chunks]
                row = lax.broadcasted_iota(jnp.int32, (2 * ds, 2 * DK), 0)
                lane = lax.broadcasted_iota(jnp.int32, (2 * ds, 2 * DK), 1)
                own_map = (lane >= DK) == (row >= ds)
                qrows = []
                for h in range(N_HEADS):
                    qf = zq[:, h * 128:(h + 1) * 128] * (DK ** -0.5)
                    qrows.append(jnp.where(own_map, jnp.concatenate([qf, qf], axis=0), 0.0))
                qall_ref[...] = jnp.concatenate(qrows, axis=0).astype(BF16)
                sm_ref[...] = jnp.full(sm_ref.shape, NEG, F32)
                sl_ref[...] = jnp.zeros(sl_ref.shape, F32)
                sacc_ref[...] = jnp.zeros(sacc_ref.shape, F32)
        return g, valid

    def chunk_scores(g):
        slot, c = g % n_slots, g % n_chunks
        mrow = lax.broadcasted_iota(jnp.int32, (nq, 128), 0)
        mcol = lax.broadcasted_iota(jnp.int32, (nq, 128), 1)
        head_mask = jnp.where(mcol % N_HEADS == mrow // (2 * ds), 0.0, NEG)
        ss = []
        for sb in range(n_sub):
            kf = kbuf[slot, sb * sub:(sb + 1) * sub, :].astype(BF16)
            s = sample_scores(kf) + jnp.tile(head_mask, (1, sub // 128))
            if sb == n_sub - 1:
                tail = jnp.where(c == n_chunks - 1, bpast_ref[...], 0.0)
                s = jnp.concatenate([s[:, :sub - rows], s[:, sub - rows:] + tail], axis=1)
            ss.append(s)
        return ss

    def chunk_update(g, valid, ss):
        slot = g % n_slots
        old = (sm_ref[...], sl_ref[...], sacc_ref[...])
        parts = [old]
        for sb in range(n_sub):
            vf = vbuf[slot, sb * sub:(sb + 1) * sub, :].astype(BF16)
            parts.append(block_softmax(ss[sb], vf))
        new = merge(parts)
        sm_ref[...] = jnp.where(valid, new[0], old[0])
        sl_ref[...] = jnp.where(valid, new[1], old[1])
        sacc_ref[...] = jnp.where(valid, new[2], old[2])

    def chunk_end(g, valid):
        @pl.when(valid)
        def _():
            count_ref[0] = g + 1

            @pl.when(g % n_chunks == n_chunks - 1)
            def _finish_request():
                req = g // n_chunks
                pad = jnp.zeros((128 - ds * N_HEADS, DV), F32)
                kn = jnp.concatenate([kn_ref[req], pad], axis=0).astype(BF16)
                vn = jnp.concatenate([vn_ref[req], pad], axis=0).astype(BF16)
                _, lf, accf = merge([(sm_ref[...], sl_ref[...], sacc_ref[...]),
                                     block_softmax(sample_scores(kn) + bnew_ref[...], vn)])
                o_all = accf / lf
                lam = _lambda(lamp_ref, lam_init)
                ga = sga_ref[req]
                for h in range(N_HEADS):
                    r0 = h * 2 * ds
                    so_ref[req, :, h * DV:(h + 1) * DV] = _head_epilogue(
                        o_all[r0:r0 + ds], o_all[r0 + ds:r0 + 2 * ds], lam, sg_ref[...],
                        ga[:, h * DV:(h + 1) * DV], lam_init)

    prep = 512

    @pl.when(i == 0)
    def _prep():
        def body(c, carry):
            r0 = pl.multiple_of(c * prep, prep)
            kt_ref[:, pl.ds(r0, prep)] = k_ref[pl.ds(r0, prep), :].T.astype(BF16)
            vx_ref[pl.ds(r0, prep), 0:DV] = v_ref[pl.ds(r0, prep), :].astype(BF16)
            return carry
        lax.fori_loop(0, seq // prep, body, 0)
        vx_ref[:, DV:2 * DV] = jnp.ones((seq, DV), BF16)

    q = q_ref[...] * (DK ** -0.5 * LOG2E)
    lane = lax.broadcasted_iota(jnp.int32, (t, 2 * DK), 1)
    qa = jnp.where(lane < DK, q, 0.0).astype(BF16)
    qb = jnp.where(lane >= DK, q, 0.0).astype(BF16)
    qs = jnp.concatenate([qa, qb], axis=0)

    m_ref[...] = jnp.full(m_ref.shape, NEG, F32)
    acc_ref[...] = jnp.zeros(acc_ref.shape, F32)

    def scores(blk, bias):
        c0 = pl.multiple_of(blk * t, t)
        s = jnp.dot(qs, kt_ref[:, pl.ds(c0, t)], preferred_element_type=F32)
        if bias is not None:
            s = (s.reshape(2, t, t) + bias[None]).reshape(2 * t, t)
        return s

    def update(blk, s):
        c0 = pl.multiple_of(blk * t, t)
        m_prev = m_ref[...]
        m_next = jnp.maximum(m_prev, jnp.max(s, axis=1, keepdims=True))
        alpha = jnp.exp2(m_prev - m_next)
        p = jnp.exp2(s - jnp.tile(m_next, (1, t // 128)))
        pv = jnp.dot(p.astype(BF16), vx_ref[pl.ds(c0, t), :], preferred_element_type=F32)
        acc_ref[...] = jnp.tile(alpha, (1, 2)) * acc_ref[...] + pv
        m_ref[...] = m_next

    def steps(blocks):
        g, valid = chunk_begin()
        ss = [scores(blk, bias) for blk, bias in blocks]
        css = chunk_scores(g)
        for (blk, _), s in zip(blocks, ss):
            update(blk, s)
        chunk_update(g, valid, css)
        chunk_end(g, valid)

    n_full = jnp.maximum(i - 1, 0)

    def group_body(jj, carry):
        steps([(group * jj + g, None) for g in range(group)])
        return carry
    lax.fori_loop(0, n_full // group, group_body, 0)

    @pl.when(i == 0)
    def _first_tile():
        steps([(0, bdiag_ref[0])])

    for rem in range(group):
        @pl.when(jnp.logical_and(i >= 1, n_full % group == rem))
        def _tail():
            left = [(i - 1 - rem + g, None) for g in range(rem)]
            steps(left + [(i - 1, bsub_ref[0]), (i, bdiag_ref[0])])

    acc = acc_ref[...]
    o_all = acc[:, 0:DV] / acc[:, DV:2 * DV]
    lam = _lambda(lamp_ref, lam_init)
    o_ref[...] = _head_epilogue(o_all[0:t], o_all[t:2 * t], lam, sg_ref[...], ga_ref[...], lam_init)

    @pl.when(last_step)
    def _drain():
        def body(_, carry):
            g, valid = chunk_begin()
            chunk_update(g, valid, chunk_scores(g))
            chunk_end(g, valid)
            return carry
        lax.fori_loop(0, total - count_ref[0], body, 0)


def _attention(z, zs, cache_k, cache_v, page_table, bsub, bdiag, bpast, bnew, lam_p, subln_g, ds, lam_init,
               t=512, group=2, ppc=8, sub=2048, n_slots=3):
    seq = z.shape[0]
    n_req, n_pages = page_table.shape
    rows = PAGE_SIZE * N_HEADS
    n_pool = cache_k.shape[0]
    nq = N_HEADS * 2 * ds
    assert 2 * ds == 8 and n_pages % ppc == 0 and (ppc * rows) % sub == 0 and sub >= rows
    ck = cache_k.reshape(n_pool, rows, 2 * DK)
    cv = cache_v.reshape(n_pool, rows, DV)
    zs3 = zs.reshape(n_req, ds, D_IN)
    kn = zs[:, D_ATTN:2 * D_ATTN].reshape(n_req, ds * N_HEADS, 2 * DK)
    vn = zs[:, 2 * D_ATTN:3 * D_ATTN].reshape(n_req, ds * N_HEADS, DV)
    kern = functools.partial(_attn_kernel, t=t, seq=seq, group=group, ds=ds, n_req=n_req, n_pages=n_pages,
                             ppc=ppc, sub=sub, lam_init=lam_init)
    once = pl.Buffered(1)
    grid_spec = pltpu.PrefetchScalarGridSpec(
        num_scalar_prefetch=1,
        grid=(N_HEADS, seq // t),
        in_specs=[
            pl.BlockSpec((t, DV), lambda h, i, pt: (i, QB + h)),
            pl.BlockSpec((seq, DV), lambda h, i, pt: (0, KB + h), pipeline_mode=once),
            pl.BlockSpec((seq, DV), lambda h, i, pt: (0, VB + h), pipeline_mode=once),
            pl.BlockSpec((t, DV), lambda h, i, pt: (i, GAB + h)),
            pl.BlockSpec((1, t, t), lambda h, i, pt: (h, 0, 0), pipeline_mode=once),
            pl.BlockSpec((1, t, t), lambda h, i, pt: (h, 0, 0), pipeline_mode=once),
            pl.BlockSpec((4, DK), lambda h, i, pt: (0, 0)),
            pl.BlockSpec((1, DV), lambda h, i, pt: (0, 0)),
            pl.BlockSpec((n_req, ds, D_ATTN), lambda h, i, pt: (0, 0, QB // N_HEADS)),
            pl.BlockSpec((n_req, ds, D_ATTN), lambda h, i, pt: (0, 0, GAB // N_HEADS)),
            pl.BlockSpec((n_req, ds * N_HEADS, 2 * DK), lambda h, i, pt: (0, 0, 0)),
            pl.BlockSpec((n_req, ds * N_HEADS, DV), lambda h, i, pt: (0, 0, 0)),
            pl.BlockSpec(memory_space=pl.ANY),
            pl.BlockSpec(memory_space=pl.ANY),
            pl.BlockSpec((nq, rows), lambda h, i, pt: (0, 0)),
            pl.BlockSpec((nq, 128), lambda h, i, pt: (0, 0)),
        ],
        out_specs=[
            pl.BlockSpec((t, DV), lambda h, i, pt: (i, h)),
            pl.BlockSpec((n_req, ds, D_ATTN), lambda h, i, pt: (0, 0, 0)),
        ],
        scratch_shapes=[
            pltpu.VMEM((DV, seq), BF16),
            pltpu.VMEM((seq, 2 * DV), BF16),
            pltpu.VMEM((2 * t, 128), F32),
            pltpu.VMEM((2 * t, 2 * DV), F32),
            pltpu.VMEM((n_slots, ppc * rows, 2 * DK), F32),
            pltpu.VMEM((n_slots, ppc * rows, DV), F32),
            pltpu.SemaphoreType.DMA((2, n_slots)),
            pltpu.VMEM((nq, 2 * DK), BF16),
            pltpu.VMEM((nq, 1), F32),
            pltpu.VMEM((nq, 1), F32),
            pltpu.VMEM((nq, DV), F32),
            pltpu.SMEM((1,), jnp.int32),
        ],
    )
    return pl.pallas_call(
        kern,
        grid_spec=grid_spec,
        out_shape=[jax.ShapeDtypeStruct((seq, D_ATTN), F32),
                   jax.ShapeDtypeStruct((n_req, ds, D_ATTN), F32)],
        compiler_params=_params(("arbitrary", "arbitrary"), vmem=ATTN_VMEM_LIMIT),
        name="attention",
    )(page_table, z, z, z, z, bsub, bdiag, lam_p, subln_g, zs3, zs3, kn, vn, ck, cv, bpast, bnew)


def _ln_swish(y, lg, lb):
    mu = jnp.mean(y, axis=-1, keepdims=True)
    yc = y - mu
    var = jnp.mean(yc * yc, axis=-1, keepdims=True)
    return _silu(yc * lax.rsqrt(var + EPS) * lg + lb)


def _pconv_kernel(ca_ref, cb_ref, wdw_ref, bdw_ref, lg_ref, lb_ref, t_ref, nb_ref, ext_ref, sh_ref, wb_ref,
                  y_ref, *, ts, rc):
    i = pl.program_id(0)
    pad = 32
    first = pad - HALO

    @pl.when(i == 0)
    def _():
        ext_ref[0:pad, :] = jnp.zeros((pad, D_CONV), F32)

    ext_ref[pad:pad + ts, :] = ca_ref[...] * jax.nn.sigmoid(cb_ref[...])
    n_sh = sh_ref.shape[1]
    for b in range(1, 8):
        sh_ref[b - 1] = ext_ref[b:b + n_sh, :]
    for j in range(CONV_WIDTH):
        wb_ref[j] = jnp.broadcast_to(wdw_ref[j:j + 1, :], (8, D_CONV))

    def chunk(r, carry):
        r0 = pl.multiple_of(r * rc, rc)
        accs = [jnp.zeros((8, D_CONV), F32) for _ in range(rc // 8)]
        for j in range(CONV_WIDTH):
            a8, b = (first + j) // 8 * 8, (first + j) % 8
            src = ext_ref if b == 0 else sh_ref.at[b - 1]
            w8 = wb_ref[j]
            for g in range(rc // 8):
                accs[g] = accs[g] + w8 * src[pl.ds(r0 + a8 + 8 * g, 8), :]
        y_ref[pl.ds(r0, rc), :] = jnp.concatenate(accs, axis=0)
        return carry

    lax.fori_loop(0, ts // rc, chunk, 0)
    y = y_ref[...] + bdw_ref[...]
    t_ref[...] = _ln_swish(y, lg_ref[...], lb_ref[...]).astype(BF16)
    nb_ref[...] = ext_ref[pad + ts - HALO:pad + ts, :]
    ext_ref[0:pad, :] = ext_ref[ts:ts + pad, :]


def _prompt_conv(z, w_dw, b_dw, ln_g, ln_b, ts=512, rc=16):
    seq = z.shape[0]
    vec = pl.BlockSpec((1, D_CONV), lambda i: (0, 0))
    return pl.pallas_call(
        functools.partial(_pconv_kernel, ts=ts, rc=rc),
        grid=(seq // ts,),
        in_specs=[
            pl.BlockSpec((ts, D_CONV), lambda i: (i, CA_BLK)),
            pl.BlockSpec((ts, D_CONV), lambda i: (i, CB_BLK)),
            pl.BlockSpec((CONV_WIDTH, D_CONV), lambda i: (0, 0)),
            vec, vec, vec,
        ],
        out_specs=[
            pl.BlockSpec((ts, D_CONV), lambda i: (i, 0)),
            pl.BlockSpec((HALO, D_CONV), lambda i: (0, 0)),
        ],
        out_shape=[
            jax.ShapeDtypeStruct((seq, D_CONV), BF16),
            jax.ShapeDtypeStruct((HALO, D_CONV), F32),
        ],
        scratch_shapes=[pltpu.VMEM((ts + 32, D_CONV), F32),
                        pltpu.VMEM((7, ts + 24, D_CONV), F32),
                        pltpu.VMEM((CONV_WIDTH, 8, D_CONV), F32),
                        pltpu.VMEM((ts, D_CONV), F32)],
        compiler_params=_params(("arbitrary",)),
        name="prompt_conv",
    )(z, z, w_dw, b_dw, ln_g, ln_b)


def _sconv_kernel(z_ref, st_ref, wdw_ref, bdw_ref, lg_ref, lb_ref, t_ref, nb_ref, ext_ref, *, ds):
    ca = z_ref[0, :, CA_BLK * D_CONV:(CA_BLK + 1) * D_CONV]
    cb = z_ref[0, :, CB_BLK * D_CONV:(CB_BLK + 1) * D_CONV]
    ext_ref[0:HALO, :] = st_ref[0]
    ext_ref[HALO:HALO + ds, :] = ca * jax.nn.sigmoid(cb)
    acc = jnp.zeros((ds, D_CONV), F32)
    for j in range(CONV_WIDTH):
        acc = acc + wdw_ref[j:j + 1, :] * ext_ref[j:j + ds, :]
    y = acc + bdw_ref[...]
    t_ref[0] = _ln_swish(y, lg_ref[...], lb_ref[...])
    nb_ref[0] = ext_ref[ds:ds + HALO, :]


def _sample_conv(zs3, state, w_dw, b_dw, ln_g, ln_b):
    db, ds, _ = zs3.shape
    vec = pl.BlockSpec((1, D_CONV), lambda b: (0, 0))
    return pl.pallas_call(
        functools.partial(_sconv_kernel, ds=ds),
        grid=(db,),
        in_specs=[
            pl.BlockSpec((1, ds, D_IN), lambda b: (b, 0, 0)),
            pl.BlockSpec((1, HALO, D_CONV), lambda b: (b, 0, 0)),
            pl.BlockSpec((CONV_WIDTH, D_CONV), lambda b: (0, 0)),
            vec, vec, vec,
        ],
        out_specs=[
            pl.BlockSpec((1, ds, D_CONV), lambda b: (b, 0, 0)),
            pl.BlockSpec((1, HALO, D_CONV), lambda b: (b, 0, 0)),
        ],
        out_shape=[
            jax.ShapeDtypeStruct((db, ds, D_CONV), F32),
            jax.ShapeDtypeStruct((db, HALO, D_CONV), F32),
        ],
        scratch_shapes=[pltpu.VMEM((HALO + 8, D_CONV), F32)],
        compiler_params=_params(("arbitrary",)),
        name="sample_conv",
    )(zs3, state, w_dw, b_dw, ln_g, ln_b)


def _outproj_kernel(x_ref, ya_ref, t_ref, gc_ref, wpw_ref, bpw_ref, wout_ref, fg_ref, y_ref):
    yc = jnp.dot(t_ref[...].astype(BF16), wpw_ref[...], preferred_element_type=F32) + bpw_ref[...]
    yc = yc * _silu(gc_ref[...])
    y = jnp.dot(ya_ref[...].astype(BF16), wout_ref[0:D_ATTN, :], preferred_element_type=F32)
    y = y + jnp.dot(yc.astype(BF16), wout_ref[D_ATTN:D_MODEL, :], preferred_element_type=F32)
    r = x_ref[...] + y
    ms = jnp.mean(r * r, axis=-1, keepdims=True)
    y_ref[...] = r * lax.rsqrt(ms + EPS) * fg_ref[...]


def _outproj(x, y_attn, t, z, w_pw_bf16, b_pw, w_out_bf16, final_g, tm):
    m = x.shape[0]
    return pl.pallas_call(
        _outproj_kernel,
        grid=(m // tm,),
        in_specs=[
            pl.BlockSpec((tm, D_MODEL), lambda i: (i, 0)),
            pl.BlockSpec((tm, D_ATTN), lambda i: (i, 0)),
            pl.BlockSpec((tm, D_CONV), lambda i: (i, 0)),
            pl.BlockSpec((tm, D_CONV), lambda i: (i, GC_BLK)),
            pl.BlockSpec((D_CONV, D_CONV), lambda i: (0, 0)),
            pl.BlockSpec((1, D_CONV), lambda i: (0, 0)),
            pl.BlockSpec((D_MODEL, D_MODEL), lambda i: (0, 0)),
            pl.BlockSpec((1, D_MODEL), lambda i: (0, 0)),
        ],
        out_specs=pl.BlockSpec((tm, D_MODEL), lambda i: (i, 0)),
        out_shape=jax.ShapeDtypeStruct((m, D_MODEL), F32),
        compiler_params=_params(("arbitrary",)),
        name="outproj",
    )(x, y_attn, t, z, w_pw_bf16, b_pw, w_out_bf16, final_g)


def kernel(x_prompt, x_sample, cache_k, cache_v, state_conv, page_table, norm_g, w_in, lam_p, subln_g,
           w_dw, b_dw, conv_ln_g, conv_ln_b, w_pw, b_pw, w_out, rel_bias, final_g):
    depth = w_in.shape[0]
    assert depth == 1, "single-layer step"
    bsz, seq, _ = x_prompt.shape
    assert bsz == 1
    db, ds, _ = x_sample.shape
    lam_init = 0.8 - 0.6 * math.exp(-0.3 * 0)
    t_attn = 512

    w_in_l = w_in[0].astype(BF16)
    w_pw_b = w_pw[0].astype(BF16)
    w_out_b = w_out[0].astype(BF16)
    g = norm_g[0][None, :]
    sg = subln_g[0][None, :]
    bdw, lg, lb, bpw = b_dw[0][None, :], conv_ln_g[0][None, :], conv_ln_b[0][None, :], b_pw[0][None, :]
    fg = final_g[None, :]

    xp = x_prompt.reshape(seq, D_MODEL)
    xs = x_sample.reshape(db * ds, D_MODEL)

    zp = _inproj(xp, g, w_in_l, tm=1024, tn=512)
    zs = _inproj(xs, g, w_in_l, tm=db * ds, tn=512)
    bsub, bdiag = _prompt_bias(rel_bias, t_attn)
    bpast, bnew = _sample_bias(rel_bias, ds)
    ya_p, ya_s = _attention(zp, zs, cache_k[0], cache_v[0], page_table, bsub, bdiag, bpast, bnew,
                            lam_p[0], sg, ds, lam_init, t=t_attn)

    t_p, nc_p = _prompt_conv(zp, w_dw[0], bdw, lg, lb)
    y_p = _outproj(xp, ya_p, t_p, zp, w_pw_b, bpw, w_out_b, fg, tm=256)

    zs3 = zs.reshape(db, ds, D_IN)
    t_s, nc_s = _sample_conv(zs3, state_conv[0], w_dw[0], bdw, lg, lb)
    y_s = _outproj(xs, ya_s.reshape(db * ds, D_ATTN), t_s.reshape(db * ds, D_CONV), zs,
                   w_pw_b, bpw, w_out_b, fg, tm=db * ds)

    return (
        y_p.reshape(1, seq, D_MODEL),
        y_s.reshape(db, ds, D_MODEL),
        zp[:, D_ATTN:2 * D_ATTN].reshape(1, 1, seq, N_HEADS, 2 * DK),
        zp[:, 2 * D_ATTN:3 * D_ATTN].reshape(1, 1, seq, N_HEADS, DV),
        nc_p.reshape(1, 1, HALO, D_CONV),
        zs[:, D_ATTN:2 * D_ATTN].reshape(1, db, ds, N_HEADS, 2 * DK),
        zs[:, 2 * D_ATTN:3 * D_ATTN].reshape(1, db, ds, N_HEADS, DV),
        nc_s.reshape(1, db, HALO, D_CONV),
    )
```

```python
import functools
import math

import jax
import jax.numpy as jnp
import numpy as np
from jax import lax
from jax.experimental import pallas as pl
from jax.experimental.pallas import tpu as pltpu

F32 = jnp.float32
BF16 = jnp.bfloat16

N_HEADS = 8
DV = 128
DK = 64
D_ATTN = N_HEADS * DV
D_CONV = 1024
D_MODEL = D_ATTN + D_CONV
D_IN = 4 * D_ATTN + 3 * D_CONV
CONV_WIDTH = 31
HALO = CONV_WIDTH - 1
N_BUCKETS = 32
MAX_DISTANCE = 128
PAGE_SIZE = 128
EPS = 1e-6
NEG = -1e30
LOG2E = math.log2(math.e)

QB, KB, VB, GAB = 0, N_HEADS, 2 * N_HEADS, 3 * N_HEADS
CA_BLK, CB_BLK, GC_BLK = 4, 5, 6

VMEM_LIMIT = 56 * 1024 * 1024
ATTN_VMEM_LIMIT = 60 * 1024 * 1024


def _params(sem, vmem=VMEM_LIMIT):
    return pltpu.CompilerParams(dimension_semantics=sem, vmem_limit_bytes=vmem)


def _bucket_thresholds():
    n = np.arange(0, MAX_DISTANCE + 1)
    max_exact = N_BUCKETS // 2
    nf = np.maximum(n, 1).astype(np.float32)
    large = max_exact + (np.log(nf / np.float32(max_exact)) / np.float32(math.log(MAX_DISTANCE / max_exact))
                         * np.float32(N_BUCKETS - max_exact)).astype(np.int32)
    large = np.minimum(large, N_BUCKETS - 1)
    bucket = np.where(n < max_exact, n, large)
    thr = []
    for b in range(1, N_BUCKETS):
        hit = np.nonzero(bucket >= b)[0]
        thr.append(int(hit[0]))
    assert bucket[-1] == N_BUCKETS - 1 and all(np.diff(bucket) >= 0)
    return tuple(thr)


_THR = _bucket_thresholds()


def _silu(x):
    return x * jax.nn.sigmoid(x)


def _inproj_kernel(x_ref, g_ref, w_ref, z_ref, kc_ref, vc_ref, h_ref, *, tm, tn):
    j = pl.program_id(1)

    @pl.when(j == 0)
    def _():
        x = x_ref[...]
        ms = jnp.mean(x * x, axis=-1, keepdims=True)
        h_ref[...] = (x * lax.rsqrt(ms + EPS) * g_ref[...]).astype(BF16)

    acc = jnp.dot(h_ref[...], w_ref[...], preferred_element_type=F32)
    z_ref[...] = acc

    heads_per_tile = tn // DV
    for out_ref, first_blk in ((kc_ref, KB), (vc_ref, VB)):
        for jt in range(D_ATTN // tn):
            @pl.when(j == first_blk // heads_per_tile + jt)
            def _():
                for c in range(heads_per_tile):
                    head = jt * heads_per_tile + c
                    out_ref[pl.ds(head, tm, stride=N_HEADS), :] = acc[:, c * DV:(c + 1) * DV]


def _inproj(x, g, w, tm, tn):
    m = x.shape[0]
    assert D_ATTN % tn == 0 and tn % DV == 0
    cache_rows = pl.BlockSpec((tm * N_HEADS, DV), lambda i, j: (i, 0))
    return pl.pallas_call(
        functools.partial(_inproj_kernel, tm=tm, tn=tn),
        grid=(m // tm, D_IN // tn),
        in_specs=[
            pl.BlockSpec((tm, D_MODEL), lambda i, j: (i, 0)),
            pl.BlockSpec((1, D_MODEL), lambda i, j: (0, 0)),
            pl.BlockSpec((D_MODEL, tn), lambda i, j: (0, j)),
        ],
        out_specs=[pl.BlockSpec((tm, tn), lambda i, j: (i, j)), cache_rows, cache_rows],
        out_shape=[jax.ShapeDtypeStruct((m, D_IN), F32),
                   jax.ShapeDtypeStruct((m * N_HEADS, 2 * DK), F32),
                   jax.ShapeDtypeStruct((m * N_HEADS, DV), F32)],
        scratch_shapes=[pltpu.VMEM((tm, D_MODEL), BF16)],
        compiler_params=_params(("arbitrary", "arbitrary")),
        name="inproj",
    )(x, g, w)


def _rel_bias_value(rb_ref, h, dist):
    val = jnp.full(dist.shape, rb_ref[0, h], F32)
    for b in range(1, N_BUCKETS):
        val = jnp.where(dist >= _THR[b - 1], rb_ref[b, h], val)
    val = val - rb_ref[N_BUCKETS - 1, h]
    return jnp.where(dist < 0, NEG, val)


def _bias_kernel(rb_ref, sub_ref, diag_ref, *, t):
    h = pl.program_id(0)
    r = lax.broadcasted_iota(jnp.int32, (t, t), 0)
    c = lax.broadcasted_iota(jnp.int32, (t, t), 1)
    diag_ref[0] = _rel_bias_value(rb_ref, h, r - c) * LOG2E
    sub_ref[0] = _rel_bias_value(rb_ref, h, r - c + t) * LOG2E


def _prompt_bias(rel_bias, t):
    shp = jax.ShapeDtypeStruct((N_HEADS, t, t), F32)
    return pl.pallas_call(
        functools.partial(_bias_kernel, t=t),
        grid=(N_HEADS,),
        in_specs=[pl.BlockSpec(memory_space=pltpu.SMEM)],
        out_specs=[pl.BlockSpec((1, t, t), lambda h: (h, 0, 0))] * 2,
        out_shape=[shp, shp],
        compiler_params=_params(("arbitrary",)),
        name="prompt_bias",
    )(rel_bias)


def _sample_bias_kernel(rb_ref, past_ref, new_ref, *, ds):
    h = pl.program_id(0)
    npast = PAGE_SIZE * N_HEADS
    row = lax.broadcasted_iota(jnp.int32, (8, npast), 0)
    col = lax.broadcasted_iota(jnp.int32, (8, npast), 1)
    past_ref[...] = _rel_bias_value(rb_ref, h, row % ds + PAGE_SIZE - col // N_HEADS)
    row = lax.broadcasted_iota(jnp.int32, (8, 128), 0)
    col = lax.broadcasted_iota(jnp.int32, (8, 128), 1)
    new = _rel_bias_value(rb_ref, h, row % ds - col // N_HEADS)
    own = jnp.logical_and(col < ds * N_HEADS, col % N_HEADS == h)
    new_ref[...] = jnp.where(own, new, NEG)


def _sample_bias(rel_bias, ds):
    npast = PAGE_SIZE * N_HEADS
    return pl.pallas_call(
        functools.partial(_sample_bias_kernel, ds=ds),
        grid=(N_HEADS,),
        in_specs=[pl.BlockSpec(memory_space=pltpu.SMEM)],
        out_specs=[pl.BlockSpec((8, npast), lambda h: (h, 0)),
                   pl.BlockSpec((8, 128), lambda h: (h, 0))],
        out_shape=[jax.ShapeDtypeStruct((N_HEADS * 8, npast), F32),
                   jax.ShapeDtypeStruct((N_HEADS * 8, 128), F32)],
        compiler_params=_params(("arbitrary",)),
        name="sample_bias",
    )(rel_bias)


def _lambda(lamp_ref, lam_init):
    lp = lamp_ref[...]
    a = jnp.sum(lp[0:1] * lp[1:2], axis=1, keepdims=True)
    b = jnp.sum(lp[2:3] * lp[3:4], axis=1, keepdims=True)
    return jnp.exp(a) - jnp.exp(b) + lam_init


def _head_epilogue(o0, o1, lam, sg, ga, lam_init):
    o = o0 - lam * o1
    ms = jnp.mean(o * o, axis=-1, keepdims=True)
    o = o * lax.rsqrt(ms + EPS) * sg * (1.0 - lam_init)
    return o * _silu(ga)


def _attn_kernel(pt_ref,
                 q_ref, k_ref, v_ref, ga_ref, bsub_ref, bdiag_ref, lamp_ref, sg_ref,
                 sq_ref, sga_ref, kn_ref, vn_ref, ck_ref, cv_ref, bpast_ref, bnew_ref,
                 o_ref, so_ref,
                 kt_ref, vx_ref, m_ref, acc_ref,
                 kbuf, vbuf, sem, qall_ref, sm_ref, sl_ref, sacc_ref, count_ref,
                 *, t, seq, group, ds, n_req, n_pages, ppc, sub, lam_init):
    hd = pl.program_id(0)
    i = pl.program_id(1)
    first_step = jnp.logical_and(hd == 0, i == 0)
    last_step = jnp.logical_and(hd == pl.num_programs(0) - 1, i == pl.num_programs(1) - 1)

    n_chunks = n_pages // ppc
    total = n_req * n_chunks
    rows = PAGE_SIZE * N_HEADS
    nq = N_HEADS * 2 * ds
    n_slots = kbuf.shape[0]
    ahead = n_slots - 1
    n_sub = ppc * rows // sub

    def page_copies(g):
        req, chunk, slot = g // n_chunks, g % n_chunks, g % n_slots
        cps = []
        for pg in range(ppc):
            page = pt_ref[req, chunk * ppc + pg]
            cps.append(pltpu.make_async_copy(
                ck_ref.at[page], kbuf.at[slot, pl.ds(pg * rows, rows), :], sem.at[0, slot]))
            cps.append(pltpu.make_async_copy(
                cv_ref.at[page], vbuf.at[slot, pl.ds(pg * rows, rows), :], sem.at[1, slot]))
        return cps

    @pl.when(first_step)
    def _prime():
        count_ref[0] = 0
        for g in range(ahead):
            for cp in page_copies(g):
                cp.start()

    def block_softmax(s, vf):
        m = jnp.max(s, axis=1, keepdims=True)
        p = jnp.exp(s - m)
        l = jnp.sum(p, axis=1, keepdims=True)
        return m, l, jnp.dot(p.astype(BF16), vf, preferred_element_type=F32)

    def merge(parts):
        m = functools.reduce(jnp.maximum, [pm for pm, _, _ in parts])
        scale = [jnp.exp(pm - m) for pm, _, _ in parts]
        l = sum(sc * pl_ for sc, (_, pl_, _) in zip(scale, parts))
        acc = sum(sc * pa for sc, (_, _, pa) in zip(scale, parts))
        return m, l, acc

    def sample_scores(kf):
        return lax.dot_general(qall_ref[...], kf, (((1,), (1,)), ((), ())), preferred_element_type=F32)

    def chunk_begin():
        g = count_ref[0]
        valid = g < total

        @pl.when(valid)
        def _():
            @pl.when(g + ahead < total)
            def _():
                for cp in page_copies(g + ahead):
                    cp.start()

            for cp in page_copies(g):
                cp.wait()

            @pl.when(g % n_chunks == 0)
            def _new_request():
                zq = sq_ref[g // n_chunks]
                row = lax.broadcasted_iota(jnp.int32, (2 * ds, 2 * DK), 0)
                lane = lax.broadcasted_iota(jnp.int32, (2 * ds, 2 * DK), 1)
                own_map = (lane >= DK) == (row >= ds)
                qrows = []
                for h in range(N_HEADS):
                    qf = zq[:, h * 128:(h + 1) * 128] * (DK ** -0.5)
                    qrows.append(jnp.where(own_map, jnp.concatenate([qf, qf], axis=0), 0.0))
                qall_ref[...] = jnp.concatenate(qrows, axis=0).astype(BF16)
                sm_ref[...] = jnp.full(sm_ref.shape, NEG, F32)
                sl_ref[...] = jnp.zeros(sl_ref.shape, F32)
                sacc_ref[...] = jnp.zeros(sacc_ref.shape, F32)
        return g, valid

    def chunk_scores(g):
        slot, c = g % n_slots, g % n_chunks
        mrow = lax.broadcasted_iota(jnp.int32, (nq, 128), 0)
        mcol = lax.broadcasted_iota(jnp.int32, (nq, 128), 1)
        head_mask = jnp.where(mcol % N_HEADS == mrow // (2 * ds), 0.0, NEG)
        ss = []
        for sb in range(n_sub):
            kf = kbuf[slot, sb * sub:(sb + 1) * sub, :].astype(BF16)
            s = sample_scores(kf) + jnp.tile(head_mask, (1, sub // 128))
            if sb == n_sub - 1:
                tail = jnp.where(c == n_chunks - 1, bpast_ref[...], 0.0)
                s = jnp.concatenate([s[:, :sub - rows], s[:, sub - rows:] + tail], axis=1)
            ss.append(s)
        return ss

    def chunk_update(g, valid, ss):
        slot = g % n_slots
        old = (sm_ref[...], sl_ref[...], sacc_ref[...])
        parts = [old]
        for sb in range(n_sub):
            vf = vbuf[slot, sb * sub:(sb + 1) * sub, :].astype(BF16)
            parts.append(block_softmax(ss[sb], vf))
        new = merge(parts)
        sm_ref[...] = jnp.where(valid, new[0], old[0])
        sl_ref[...] = jnp.where(valid, new[1], old[1])
        sacc_ref[...] = jnp.where(valid, new[2], old[2])

    def chunk_end(g, valid):
        @pl.when(valid)
        def _():
            count_ref[0] = g + 1

            @pl.when(g % n_chunks == n_chunks - 1)
            def _finish_request():
                req = g // n_chunks
                pad = jnp.zeros((128 - ds * N_HEADS, DV), F32)
                kn = jnp.concatenate([kn_ref[req], pad], axis=0).astype(BF16)
                vn = jnp.concatenate([vn_ref[req], pad], axis=0).astype(BF16)
                _, lf, accf = merge([(sm_ref[...], sl_ref[...], sacc_ref[...]),
                                     block_softmax(sample_scores(kn) + bnew_ref[...], vn)])
                o_all = accf / lf
                lam = _lambda(lamp_ref, lam_init)
                ga = sga_ref[req]
                for h in range(N_HEADS):
                    r0 = h * 2 * ds
                    so_ref[req, :, h * DV:(h + 1) * DV] = _head_epilogue(
                        o_all[r0:r0 + ds], o_all[r0 + ds:r0 + 2 * ds], lam, sg_ref[...],
                        ga[:, h * DV:(h + 1) * DV], lam_init)

    prep = 512

    @pl.when(i == 0)
    def _prep():
        def body(c, carry):
            r0 = pl.multiple_of(c * prep, prep)
            kt_ref[:, pl.ds(r0, prep)] = k_ref[pl.ds(r0, prep), :].T.astype(BF16)
            vx_ref[pl.ds(r0, prep), 0:DV] = v_ref[pl.ds(r0, prep), :].astype(BF16)
            return carry
        lax.fori_loop(0, seq // prep, body, 0)
        vx_ref[:, DV:2 * DV] = jnp.ones((seq, DV), BF16)

    q = q_ref[...] * (DK ** -0.5 * LOG2E)
    lane = lax.broadcasted_iota(jnp.int32, (t, 2 * DK), 1)
    qa = jnp.where(lane < DK, q, 0.0).astype(BF16)
    qb = jnp.where(lane >= DK, q, 0.0).astype(BF16)
    qs = jnp.concatenate([qa, qb], axis=0)

    m_ref[...] = jnp.full(m_ref.shape, NEG, F32)
    acc_ref[...] = jnp.zeros(acc_ref.shape, F32)

    def scores(blk, bias):
        c0 = pl.multiple_of(blk * t, t)
        s = jnp.dot(qs, kt_ref[:, pl.ds(c0, t)], preferred_element_type=F32)
        if bias is not None:
            s = (s.reshape(2, t, t) + bias[None]).reshape(2 * t, t)
        return s

    def update(blk, s):
        c0 = pl.multiple_of(blk * t, t)
        m_prev = m_ref[...]
        m_next = jnp.maximum(m_prev, jnp.max(s, axis=1, keepdims=True))
        alpha = jnp.exp2(m_prev - m_next)
        p = jnp.exp2(s - jnp.tile(m_next, (1, t // 128)))
        pv = jnp.dot(p.astype(BF16), vx_ref[pl.ds(c0, t), :], preferred_element_type=F32)
        acc_ref[...] = jnp.tile(alpha, (1, 2)) * acc_ref[...] + pv
        m_ref[...] = m_next

    def steps(blocks):
        g, valid = chunk_begin()
        ss = [scores(blk, bias) for blk, bias in blocks]
        css = chunk_scores(g)
        for (blk, _), s in zip(blocks, ss):
            update(blk, s)
        chunk_update(g, valid, css)
        chunk_end(g, valid)

    n_full = jnp.maximum(i - 1, 0)

    def group_body(jj, carry):
        steps([(group * jj + g, None) for g in range(group)])
        return carry
    lax.fori_loop(0, n_full // group, group_body, 0)

    @pl.when(i == 0)
    def _first_tile():
        steps([(0, bdiag_ref[0])])

    for rem in range(group):
        @pl.when(jnp.logical_and(i >= 1, n_full % group == rem))
        def _tail():
            left = [(i - 1 - rem + g, None) for g in range(rem)]
            steps(left + [(i - 1, bsub_ref[0]), (i, bdiag_ref[0])])

    acc = acc_ref[...]
    o_all = acc[:, 0:DV] / acc[:, DV:2 * DV]
    lam = _lambda(lamp_ref, lam_init)
    o_ref[...] = _head_epilogue(o_all[0:t], o_all[t:2 * t], lam, sg_ref[...], ga_ref[...], lam_init)

    @pl.when(last_step)
    def _drain():
        def body(_, carry):
            g, valid = chunk_begin()
            chunk_update(g, valid, chunk_scores(g))
            chunk_end(g, valid)
            return carry
        lax.fori_loop(0, total - count_ref[0], body, 0)


def _attention(z, zs, kc_s, vc_s, cache_k, cache_v, page_table, bsub, bdiag, bpast, bnew, lam_p, subln_g, ds,
               lam_init, t=512, group=2, ppc=8, sub=2048, n_slots=3):
    seq = z.shape[0]
    n_req, n_pages = page_table.shape
    rows = PAGE_SIZE * N_HEADS
    n_pool = cache_k.shape[0]
    nq = N_HEADS * 2 * ds
    assert 2 * ds == 8 and n_pages % ppc == 0 and (ppc * rows) % sub == 0 and sub >= rows
    ck = cache_k.reshape(n_pool, rows, 2 * DK)
    cv = cache_v.reshape(n_pool, rows, DV)
    zs3 = zs.reshape(n_req, ds, D_IN)
    kn = kc_s.reshape(n_req, ds * N_HEADS, 2 * DK)
    vn = vc_s.reshape(n_req, ds * N_HEADS, DV)
    kern = functools.partial(_attn_kernel, t=t, seq=seq, group=group, ds=ds, n_req=n_req, n_pages=n_pages,
                             ppc=ppc, sub=sub, lam_init=lam_init)
    once = pl.Buffered(1)
    grid_spec = pltpu.PrefetchScalarGridSpec(
        num_scalar_prefetch=1,
        grid=(N_HEADS, seq // t),
        in_specs=[
            pl.BlockSpec((t, DV), lambda h, i, pt: (i, QB + h)),
            pl.BlockSpec((seq, DV), lambda h, i, pt: (0, KB + h), pipeline_mode=once),
            pl.BlockSpec((seq, DV), lambda h, i, pt: (0, VB + h), pipeline_mode=once),
            pl.BlockSpec((t, DV), lambda h, i, pt: (i, GAB + h)),
            pl.BlockSpec((1, t, t), lambda h, i, pt: (h, 0, 0), pipeline_mode=once),
            pl.BlockSpec((1, t, t), lambda h, i, pt: (h, 0, 0), pipeline_mode=once),
            pl.BlockSpec((4, DK), lambda h, i, pt: (0, 0)),
            pl.BlockSpec((1, DV), lambda h, i, pt: (0, 0)),
            pl.BlockSpec((n_req, ds, D_ATTN), lambda h, i, pt: (0, 0, QB // N_HEADS)),
            pl.BlockSpec((n_req, ds, D_ATTN), lambda h, i, pt: (0, 0, GAB // N_HEADS)),
            pl.BlockSpec((n_req, ds * N_HEADS, 2 * DK), lambda h, i, pt: (0, 0, 0)),
            pl.BlockSpec((n_req, ds * N_HEADS, DV), lambda h, i, pt: (0, 0, 0)),
            pl.BlockSpec(memory_space=pl.ANY),
            pl.BlockSpec(memory_space=pl.ANY),
            pl.BlockSpec((nq, rows), lambda h, i, pt: (0, 0)),
            pl.BlockSpec((nq, 128), lambda h, i, pt: (0, 0)),
        ],
        out_specs=[
            pl.BlockSpec((t, DV), lambda h, i, pt: (i, h)),
            pl.BlockSpec((n_req, ds, D_ATTN), lambda h, i, pt: (0, 0, 0)),
        ],
        scratch_shapes=[
            pltpu.VMEM((DV, seq), BF16),
            pltpu.VMEM((seq, 2 * DV), BF16),
            pltpu.VMEM((2 * t, 128), F32),
            pltpu.VMEM((2 * t, 2 * DV), F32),
            pltpu.VMEM((n_slots, ppc * rows, 2 * DK), F32),
            pltpu.VMEM((n_slots, ppc * rows, DV), F32),
            pltpu.SemaphoreType.DMA((2, n_slots)),
            pltpu.VMEM((nq, 2 * DK), BF16),
            pltpu.VMEM((nq, 1), F32),
            pltpu.VMEM((nq, 1), F32),
            pltpu.VMEM((nq, DV), F32),
            pltpu.SMEM((1,), jnp.int32),
        ],
    )
    return pl.pallas_call(
        kern,
        grid_spec=grid_spec,
        out_shape=[jax.ShapeDtypeStruct((seq, D_ATTN), F32),
                   jax.ShapeDtypeStruct((n_req, ds, D_ATTN), F32)],
        compiler_params=_params(("arbitrary", "arbitrary"), vmem=ATTN_VMEM_LIMIT),
        name="attention",
    )(page_table, z, z, z, z, bsub, bdiag, lam_p, subln_g, zs3, zs3, kn, vn, ck, cv, bpast, bnew)


def _ln_swish(y, lg, lb):
    mu = jnp.mean(y, axis=-1, keepdims=True)
    yc = y - mu
    var = jnp.mean(yc * yc, axis=-1, keepdims=True)
    return _silu(yc * lax.rsqrt(var + EPS) * lg + lb)


def _pconv_kernel(ca_ref, cb_ref, wdw_ref, bdw_ref, lg_ref, lb_ref, t_ref, nb_ref, ext_ref, sh_ref, wb_ref,
                  y_ref, *, ts, rc):
    i = pl.program_id(0)
    pad = 32
    first = pad - HALO

    @pl.when(i == 0)
    def _():
        ext_ref[0:pad, :] = jnp.zeros((pad, D_CONV), F32)

    ext_ref[pad:pad + ts, :] = ca_ref[...] * jax.nn.sigmoid(cb_ref[...])
    n_sh = sh_ref.shape[1]
    for b in range(1, 8):
        sh_ref[b - 1] = ext_ref[b:b + n_sh, :]
    for j in range(CONV_WIDTH):
        wb_ref[j] = jnp.broadcast_to(wdw_ref[j:j + 1, :], (8, D_CONV))

    for lb in range(D_CONV // 128):
        lanes = slice(lb * 128, (lb + 1) * 128)
        ws = [wb_ref[j, :, lanes] for j in range(CONV_WIDTH)]

        def chunk(r, carry, lanes=lanes, ws=ws):
            r0 = pl.multiple_of(r * rc, rc)
            accs = [jnp.zeros((8, 128), F32) for _ in range(rc // 8)]
            for j in range(CONV_WIDTH):
                a8, b = (first + j) // 8 * 8, (first + j) % 8
                src = ext_ref if b == 0 else sh_ref.at[b - 1]
                for g in range(rc // 8):
                    accs[g] = accs[g] + ws[j] * src[pl.ds(r0 + a8 + 8 * g, 8), lanes]
            y_ref[pl.ds(r0, rc), lanes] = jnp.concatenate(accs, axis=0)
            return carry

        lax.fori_loop(0, ts // rc, chunk, 0)
    y = y_ref[...] + bdw_ref[...]
    t_ref[...] = _ln_swish(y, lg_ref[...], lb_ref[...]).astype(BF16)
    nb_ref[...] = ext_ref[pad + ts - HALO:pad + ts, :]
    ext_ref[0:pad, :] = ext_ref[ts:ts + pad, :]


def _prompt_conv(z, w_dw, b_dw, ln_g, ln_b, ts=512, rc=64):
    seq = z.shape[0]
    vec = pl.BlockSpec((1, D_CONV), lambda i: (0, 0))
    return pl.pallas_call(
        functools.partial(_pconv_kernel, ts=ts, rc=rc),
        grid=(seq // ts,),
        in_specs=[
            pl.BlockSpec((ts, D_CONV), lambda i: (i, CA_BLK)),
            pl.BlockSpec((ts, D_CONV), lambda i: (i, CB_BLK)),
            pl.BlockSpec((CONV_WIDTH, D_CONV), lambda i: (0, 0)),
            vec, vec, vec,
        ],
        out_specs=[
            pl.BlockSpec((ts, D_CONV), lambda i: (i, 0)),
            pl.BlockSpec((HALO, D_CONV), lambda i: (0, 0)),
        ],
        out_shape=[
            jax.ShapeDtypeStruct((seq, D_CONV), BF16),
            jax.ShapeDtypeStruct((HALO, D_CONV), F32),
        ],
        scratch_shapes=[pltpu.VMEM((ts + 32, D_CONV), F32),
                        pltpu.VMEM((7, ts + 24, D_CONV), F32),
                        pltpu.VMEM((CONV_WIDTH, 8, D_CONV), F32),
                        pltpu.VMEM((ts, D_CONV), F32)],
        compiler_params=_params(("arbitrary",)),
        name="prompt_conv",
    )(z, z, w_dw, b_dw, ln_g, ln_b)


def _sconv_kernel(z_ref, st_ref, wdw_ref, bdw_ref, lg_ref, lb_ref, t_ref, nb_ref, ext_ref, *, ds):
    ca = z_ref[0, :, CA_BLK * D_CONV:(CA_BLK + 1) * D_CONV]
    cb = z_ref[0, :, CB_BLK * D_CONV:(CB_BLK + 1) * D_CONV]
    ext_ref[0:HALO, :] = st_ref[0]
    ext_ref[HALO:HALO + ds, :] = ca * jax.nn.sigmoid(cb)
    acc = jnp.zeros((ds, D_CONV), F32)
    for j in range(CONV_WIDTH):
        acc = acc + wdw_ref[j:j + 1, :] * ext_ref[j:j + ds, :]
    y = acc + bdw_ref[...]
    t_ref[0] = _ln_swish(y, lg_ref[...], lb_ref[...])
    nb_ref[0] = ext_ref[ds:ds + HALO, :]


def _sample_conv(zs3, state, w_dw, b_dw, ln_g, ln_b):
    db, ds, _ = zs3.shape
    vec = pl.BlockSpec((1, D_CONV), lambda b: (0, 0))
    return pl.pallas_call(
        functools.partial(_sconv_kernel, ds=ds),
        grid=(db,),
        in_specs=[
            pl.BlockSpec((1, ds, D_IN), lambda b: (b, 0, 0)),
            pl.BlockSpec((1, HALO, D_CONV), lambda b: (b, 0, 0)),
            pl.BlockSpec((CONV_WIDTH, D_CONV), lambda b: (0, 0)),
            vec, vec, vec,
        ],
        out_specs=[
            pl.BlockSpec((1, ds, D_CONV), lambda b: (b, 0, 0)),
            pl.BlockSpec((1, HALO, D_CONV), lambda b: (b, 0, 0)),
        ],
        out_shape=[
            jax.ShapeDtypeStruct((db, ds, D_CONV), F32),
            jax.ShapeDtypeStruct((db, HALO, D_CONV), F32),
        ],
        scratch_shapes=[pltpu.VMEM((HALO + 8, D_CONV), F32)],
        compiler_params=_params(("arbitrary",)),
        name="sample_conv",
    )(zs3, state, w_dw, b_dw, ln_g, ln_b)


def _outproj_kernel(x_ref, ya_ref, t_ref, gc_ref, wpw_ref, bpw_ref, wout_ref, fg_ref, y_ref):
    yc = jnp.dot(t_ref[...].astype(BF16), wpw_ref[...], preferred_element_type=F32) + bpw_ref[...]
    yc = yc * _silu(gc_ref[...])
    y = jnp.dot(ya_ref[...].astype(BF16), wout_ref[0:D_ATTN, :], preferred_element_type=F32)
    y = y + jnp.dot(yc.astype(BF16), wout_ref[D_ATTN:D_MODEL, :], preferred_element_type=F32)
    r = x_ref[...] + y
    ms = jnp.mean(r * r, axis=-1, keepdims=True)
    y_ref[...] = r * lax.rsqrt(ms + EPS) * fg_ref[...]


def _outproj(x, y_attn, t, z, w_pw_bf16, b_pw, w_out_bf16, final_g, tm):
    m = x.shape[0]
    return pl.pallas_call(
        _outproj_kernel,
        grid=(m // tm,),
        in_specs=[
            pl.BlockSpec((tm, D_MODEL), lambda i: (i, 0)),
            pl.BlockSpec((tm, D_ATTN), lambda i: (i, 0)),
            pl.BlockSpec((tm, D_CONV), lambda i: (i, 0)),
            pl.BlockSpec((tm, D_CONV), lambda i: (i, GC_BLK)),
            pl.BlockSpec((D_CONV, D_CONV), lambda i: (0, 0)),
            pl.BlockSpec((1, D_CONV), lambda i: (0, 0)),
            pl.BlockSpec((D_MODEL, D_MODEL), lambda i: (0, 0)),
            pl.BlockSpec((1, D_MODEL), lambda i: (0, 0)),
        ],
        out_specs=pl.BlockSpec((tm, D_MODEL), lambda i: (i, 0)),
        out_shape=jax.ShapeDtypeStruct((m, D_MODEL), F32),
        compiler_params=_params(("arbitrary",)),
        name="outproj",
    )(x, y_attn, t, z, w_pw_bf16, b_pw, w_out_bf16, final_g)


def kernel(x_prompt, x_sample, cache_k, cache_v, state_conv, page_table, norm_g, w_in, lam_p, subln_g,
           w_dw, b_dw, conv_ln_g, conv_ln_b, w_pw, b_pw, w_out, rel_bias, final_g):
    depth = w_in.shape[0]
    assert depth == 1, "single-layer step"
    bsz, seq, _ = x_prompt.shape
    assert bsz == 1
    db, ds, _ = x_sample.shape
    lam_init = 0.8 - 0.6 * math.exp(-0.3 * 0)
    t_attn = 512

    w_in_l = w_in[0].astype(BF16)
    w_pw_b = w_pw[0].astype(BF16)
    w_out_b = w_out[0].astype(BF16)
    g = norm_g[0][None, :]
    sg = subln_g[0][None, :]
    bdw, lg, lb, bpw = b_dw[0][None, :], conv_ln_g[0][None, :], conv_ln_b[0][None, :], b_pw[0][None, :]
    fg = final_g[None, :]

    xp = x_prompt.reshape(seq, D_MODEL)
    xs = x_sample.reshape(db * ds, D_MODEL)

    zp, kc_p, vc_p = _inproj(xp, g, w_in_l, tm=1024, tn=512)
    zs, kc_s, vc_s = _inproj(xs, g, w_in_l, tm=db * ds, tn=512)
    bsub, bdiag = _prompt_bias(rel_bias, t_attn)
    bpast, bnew = _sample_bias(rel_bias, ds)
    ya_p, ya_s = _attention(zp, zs, kc_s, vc_s, cache_k[0], cache_v[0], page_table, bsub, bdiag, bpast, bnew,
                            lam_p[0], sg, ds, lam_init, t=t_attn)

    t_p, nc_p = _prompt_conv(zp, w_dw[0], bdw, lg, lb)
    y_p = _outproj(xp, ya_p, t_p, zp, w_pw_b, bpw, w_out_b, fg, tm=256)

    zs3 = zs.reshape(db, ds, D_IN)
    t_s, nc_s = _sample_conv(zs3, state_conv[0], w_dw[0], bdw, lg, lb)
    y_s = _outproj(xs, ya_s.reshape(db * ds, D_ATTN), t_s.reshape(db * ds, D_CONV), zs,
                   w_pw_b, bpw, w_out_b, fg, tm=db * ds)

    return (
        y_p.reshape(1, seq, D_MODEL),
        y_s.reshape(db, ds, D_MODEL),
        kc_p.reshape(1, 1, seq, N_HEADS, 2 * DK),
        vc_p.reshape(1, 1, seq, N_HEADS, DV),
        nc_p.reshape(1, 1, HALO, D_CONV),
        kc_s.reshape(1, db, ds, N_HEADS, 2 * DK),
        vc_s.reshape(1, db, ds, N_HEADS, DV),
        nc_s.reshape(1, db, HALO, D_CONV),
    )
```

```python
import functools
import math

import jax
import jax.numpy as jnp
import numpy as np
from jax import lax
from jax.experimental import pallas as pl
from jax.experimental.pallas import tpu as pltpu

F32 = jnp.float32
BF16 = jnp.bfloat16

N_HEADS = 8
DV = 128
DK = 64
D_ATTN = N_HEADS * DV
D_CONV = 1024
D_MODEL = D_ATTN + D_CONV
D_IN = 4 * D_ATTN + 3 * D_CONV
CONV_WIDTH = 31
HALO = CONV_WIDTH - 1
N_BUCKETS = 32
MAX_DISTANCE = 128
PAGE_SIZE = 128
EPS = 1e-6
NEG = -1e30
LOG2E = math.log2(math.e)

QB, KB, VB, GAB = 0, N_HEADS, 2 * N_HEADS, 3 * N_HEADS
CA_BLK, CB_BLK, GC_BLK = 4, 5, 6

VMEM_LIMIT = 56 * 1024 * 1024
ATTN_VMEM_LIMIT = 60 * 1024 * 1024


def _params(sem, vmem=VMEM_LIMIT):
    return pltpu.CompilerParams(dimension_semantics=sem, vmem_limit_bytes=vmem)


def _bucket_thresholds():
    n = np.arange(0, MAX_DISTANCE + 1)
    max_exact = N_BUCKETS // 2
    nf = np.maximum(n, 1).astype(np.float32)
    large = max_exact + (np.log(nf / np.float32(max_exact)) / np.float32(math.log(MAX_DISTANCE / max_exact))
                         * np.float32(N_BUCKETS - max_exact)).astype(np.int32)
    large = np.minimum(large, N_BUCKETS - 1)
    bucket = np.where(n < max_exact, n, large)
    thr = []
    for b in range(1, N_BUCKETS):
        hit = np.nonzero(bucket >= b)[0]
        thr.append(int(hit[0]))
    assert bucket[-1] == N_BUCKETS - 1 and all(np.diff(bucket) >= 0)
    return tuple(thr)


_THR = _bucket_thresholds()


def _silu(x):
    return x * jax.nn.sigmoid(x)


def _inproj_kernel(x_ref, g_ref, w_ref, z_ref, kc_ref, vc_ref, *rest, tm, tn, emit_w_bf16):
    j = pl.program_id(1)
    h_ref = rest[-1]

    @pl.when(j == 0)
    def _():
        x = x_ref[...]
        ms = jnp.mean(x * x, axis=-1, keepdims=True)
        h_ref[...] = (x * lax.rsqrt(ms + EPS) * g_ref[...]).astype(BF16)

    w = w_ref[...]
    if emit_w_bf16:
        w = w.astype(BF16)
        rest[0][...] = w
    acc = jnp.dot(h_ref[...], w, preferred_element_type=F32)
    z_ref[...] = acc

    heads_per_tile = tn // DV
    for out_ref, first_blk in ((kc_ref, KB), (vc_ref, VB)):
        for jt in range(D_ATTN // tn):
            @pl.when(j == first_blk // heads_per_tile + jt)
            def _():
                for c in range(heads_per_tile):
                    head = jt * heads_per_tile + c
                    out_ref[pl.ds(head, tm, stride=N_HEADS), :] = acc[:, c * DV:(c + 1) * DV]


def _inproj(x, g, w, tm, tn):
    m = x.shape[0]
    emit_w_bf16 = w.dtype == F32
    assert D_ATTN % tn == 0 and tn % DV == 0 and (m == tm or not emit_w_bf16)
    cache_rows = pl.BlockSpec((tm * N_HEADS, DV), lambda i, j: (i, 0))
    w_tile = pl.BlockSpec((D_MODEL, tn), lambda i, j: (0, j))
    out_specs = [pl.BlockSpec((tm, tn), lambda i, j: (i, j)), cache_rows, cache_rows]
    out_shape = [jax.ShapeDtypeStruct((m, D_IN), F32),
                 jax.ShapeDtypeStruct((m * N_HEADS, 2 * DK), F32),
                 jax.ShapeDtypeStruct((m * N_HEADS, DV), F32)]
    if emit_w_bf16:
        out_specs.append(w_tile)
        out_shape.append(jax.ShapeDtypeStruct(w.shape, BF16))
    return pl.pallas_call(
        functools.partial(_inproj_kernel, tm=tm, tn=tn, emit_w_bf16=emit_w_bf16),
        grid=(m // tm, D_IN // tn),
        in_specs=[
            pl.BlockSpec((tm, D_MODEL), lambda i, j: (i, 0)),
            pl.BlockSpec((1, D_MODEL), lambda i, j: (0, 0)),
            w_tile,
        ],
        out_specs=out_specs,
        out_shape=out_shape,
        scratch_shapes=[pltpu.VMEM((tm, D_MODEL), BF16)],
        compiler_params=_params(("arbitrary", "arbitrary")),
        name="inproj",
    )(x, g, w)


def _rel_bias_value(rb_ref, h, dist):
    val = jnp.full(dist.shape, rb_ref[0, h], F32)
    for b in range(1, N_BUCKETS):
        val = jnp.where(dist >= _THR[b - 1], rb_ref[b, h], val)
    val = val - rb_ref[N_BUCKETS - 1, h]
    return jnp.where(dist < 0, NEG, val)


def _bias_kernel(rb_ref, sub_ref, diag_ref, *, t):
    h = pl.program_id(0)
    r = lax.broadcasted_iota(jnp.int32, (t, t), 0)
    c = lax.broadcasted_iota(jnp.int32, (t, t), 1)
    diag_ref[0] = _rel_bias_value(rb_ref, h, r - c) * LOG2E
    sub_ref[0] = _rel_bias_value(rb_ref, h, r - c + t) * LOG2E


def _prompt_bias(rel_bias, t):
    shp = jax.ShapeDtypeStruct((N_HEADS, t, t), F32)
    return pl.pallas_call(
        functools.partial(_bias_kernel, t=t),
        grid=(N_HEADS,),
        in_specs=[pl.BlockSpec(memory_space=pltpu.SMEM)],
        out_specs=[pl.BlockSpec((1, t, t), lambda h: (h, 0, 0))] * 2,
        out_shape=[shp, shp],
        compiler_params=_params(("arbitrary",)),
        name="prompt_bias",
    )(rel_bias)


def _sample_bias_kernel(rb_ref, past_ref, new_ref, *, ds):
    h = pl.program_id(0)
    npast = PAGE_SIZE * N_HEADS
    row = lax.broadcasted_iota(jnp.int32, (8, npast), 0)
    col = lax.broadcasted_iota(jnp.int32, (8, npast), 1)
    past_ref[...] = _rel_bias_value(rb_ref, h, row % ds + PAGE_SIZE - col // N_HEADS)
    row = lax.broadcasted_iota(jnp.int32, (8, 128), 0)
    col = lax.broadcasted_iota(jnp.int32, (8, 128), 1)
    new = _rel_bias_value(rb_ref, h, row % ds - col // N_HEADS)
    own = jnp.logical_and(col < ds * N_HEADS, col % N_HEADS == h)
    new_ref[...] = jnp.where(own, new, NEG)


def _sample_bias(rel_bias, ds):
    npast = PAGE_SIZE * N_HEADS
    return pl.pallas_call(
        functools.partial(_sample_bias_kernel, ds=ds),
        grid=(N_HEADS,),
        in_specs=[pl.BlockSpec(memory_space=pltpu.SMEM)],
        out_specs=[pl.BlockSpec((8, npast), lambda h: (h, 0)),
                   pl.BlockSpec((8, 128), lambda h: (h, 0))],
        out_shape=[jax.ShapeDtypeStruct((N_HEADS * 8, npast), F32),
                   jax.ShapeDtypeStruct((N_HEADS * 8, 128), F32)],
        compiler_params=_params(("arbitrary",)),
        name="sample_bias",
    )(rel_bias)


def _lambda(lamp_ref, lam_init):
    lp = lamp_ref[...]
    a = jnp.sum(lp[0:1] * lp[1:2], axis=1, keepdims=True)
    b = jnp.sum(lp[2:3] * lp[3:4], axis=1, keepdims=True)
    return jnp.exp(a) - jnp.exp(b) + lam_init


def _head_epilogue(o0, o1, lam, sg, ga, lam_init):
    o = o0 - lam * o1
    ms = jnp.mean(o * o, axis=-1, keepdims=True)
    o = o * lax.rsqrt(ms + EPS) * sg * (1.0 - lam_init)
    return o * _silu(ga)


def _attn_kernel(pt_ref,
                 q_ref, k_ref, v_ref, ga_ref, bsub_ref, bdiag_ref, lamp_ref, sg_ref,
                 sq_ref, sga_ref, kn_ref, vn_ref, ck_ref, cv_ref, bpast_ref, bnew_ref,
                 o_ref, so_ref,
                 kt_ref, vx_ref, m_ref, acc_ref,
                 kbuf, vbuf, sem, qall_ref, sm_ref, sl_ref, sacc_ref, count_ref,
                 *, t, seq, group, ds, n_req, n_pages, ppc, sub, lam_init):
    hd = pl.program_id(0)
    i = pl.program_id(1)
    first_step = jnp.logical_and(hd == 0, i == 0)
    last_step = jnp.logical_and(hd == pl.num_programs(0) - 1, i == pl.num_programs(1) - 1)

    n_chunks = n_pages // ppc
    total = n_req * n_chunks
    rows = PAGE_SIZE * N_HEADS
    nq = N_HEADS * 2 * ds
    n_slots = kbuf.shape[0]
    ahead = n_slots - 1
    n_sub = ppc * rows // sub

    def page_copies(g):
        req, chunk, slot = g // n_chunks, g % n_chunks, g % n_slots
        cps = []
        for pg in range(ppc):
            page = pt_ref[req, chunk * ppc + pg]
            cps.append(pltpu.make_async_copy(
                ck_ref.at[page], kbuf.at[slot, pl.ds(pg * rows, rows), :], sem.at[0, slot]))
            cps.append(pltpu.make_async_copy(
                cv_ref.at[page], vbuf.at[slot, pl.ds(pg * rows, rows), :], sem.at[1, slot]))
        return cps

    @pl.when(first_step)
    def _prime():
        count_ref[0] = 0
        for g in range(ahead):
            for cp in page_copies(g):
                cp.start()

    def block_softmax(s, vf):
        m = jnp.max(s, axis=1, keepdims=True)
        p = jnp.exp(s - m)
        l = jnp.sum(p, axis=1, keepdims=True)
        return m, l, jnp.dot(p.astype(BF16), vf, preferred_element_type=F32)

    def merge(parts):
        m = functools.reduce(jnp.maximum, [pm for pm, _, _ in parts])
        scale = [jnp.exp(pm - m) for pm, _, _ in parts]
        l = sum(sc * pl_ for sc, (_, pl_, _) in zip(scale, parts))
        acc = sum(sc * pa for sc, (_, _, pa) in zip(scale, parts))
        return m, l, acc

    def sample_scores(kf):
        return lax.dot_general(qall_ref[...], kf, (((1,), (1,)), ((), ())), preferred_element_type=F32)

    def chunk_begin():
        g = count_ref[0]
        valid = g < total

        @pl.when(valid)
        def _():
            @pl.when(g + ahead < total)
            def _():
                for cp in page_copies(g + ahead):
                    cp.start()

            for cp in page_copies(g):
                cp.wait()

            @pl.when(g % n_chunks == 0)
            def _new_request():
                zq = sq_ref[g // n_chunks]
                row = lax.broadcasted_iota(jnp.int32, (2 * ds, 2 * DK), 0)
                lane = lax.broadcasted_iota(jnp.int32, (2 * ds, 2 * DK), 1)
                own_map = (lane >= DK) == (row >= ds)
                qrows = []
                for h in range(N_HEADS):
                    qf = zq[:, h * 128:(h + 1) * 128] * (DK ** -0.5)
                    qrows.append(jnp.where(own_map, jnp.concatenate([qf, qf], axis=0), 0.0))
                qall_ref[...] = jnp.concatenate(qrows, axis=0).astype(BF16)
                sm_ref[...] = jnp.full(sm_ref.shape, NEG, F32)
                sl_ref[...] = jnp.zeros(sl_ref.shape, F32)
                sacc_ref[...] = jnp.zeros(sacc_ref.shape, F32)
        return g, valid

    def chunk_score_block(g, sb):
        slot, c = g % n_slots, g % n_chunks
        mrow = lax.broadcasted_iota(jnp.int32, (nq, 128), 0)
        mcol = lax.broadcasted_iota(jnp.int32, (nq, 128), 1)
        head_mask = jnp.where(mcol % N_HEADS == mrow // (2 * ds), 0.0, NEG)
        kf = kbuf[slot, sb * sub:(sb + 1) * sub, :].astype(BF16)
        s = sample_scores(kf) + jnp.tile(head_mask, (1, sub // 128))
        if sb == n_sub - 1:
            tail = jnp.where(c == n_chunks - 1, bpast_ref[...], 0.0)
            if sub > rows:
                s = jnp.concatenate([s[:, :sub - rows], s[:, sub - rows:] + tail], axis=1)
            else:
                s = s + tail
        return s

    def chunk_softmax_block(g, sb, s):
        vf = vbuf[g % n_slots, sb * sub:(sb + 1) * sub, :].astype(BF16)
        return block_softmax(s, vf)

    def chunk_merge(valid, parts):
        old = (sm_ref[...], sl_ref[...], sacc_ref[...])
        new = merge([old] + parts)
        sm_ref[...] = jnp.where(valid, new[0], old[0])
        sl_ref[...] = jnp.where(valid, new[1], old[1])
        sacc_ref[...] = jnp.where(valid, new[2], old[2])

    def chunk_end(g, valid):
        @pl.when(valid)
        def _():
            count_ref[0] = g + 1

            @pl.when(g % n_chunks == n_chunks - 1)
            def _finish_request():
                req = g // n_chunks
                pad = jnp.zeros((128 - ds * N_HEADS, DV), F32)
                kn = jnp.concatenate([kn_ref[req], pad], axis=0).astype(BF16)
                vn = jnp.concatenate([vn_ref[req], pad], axis=0).astype(BF16)
                _, lf, accf = merge([(sm_ref[...], sl_ref[...], sacc_ref[...]),
                                     block_softmax(sample_scores(kn) + bnew_ref[...], vn)])
                o_all = accf / lf
                lam = _lambda(lamp_ref, lam_init)
                ga = sga_ref[req]
                for h in range(N_HEADS):
                    r0 = h * 2 * ds
                    so_ref[req, :, h * DV:(h + 1) * DV] = _head_epilogue(
                        o_all[r0:r0 + ds], o_all[r0 + ds:r0 + 2 * ds], lam, sg_ref[...],
                        ga[:, h * DV:(h + 1) * DV], lam_init)

    prep = 512

    @pl.when(i == 0)
    def _prep():
        def body(c, carry):
            r0 = pl.multiple_of(c * prep, prep)
            kt_ref[:, pl.ds(r0, prep)] = k_ref[pl.ds(r0, prep), :].T.astype(BF16)
            vx_ref[pl.ds(r0, prep), 0:DV] = v_ref[pl.ds(r0, prep), :].astype(BF16)
            return carry
        lax.fori_loop(0, seq // prep, body, 0)
        vx_ref[:, DV:2 * DV] = jnp.ones((seq, DV), BF16)

    q = q_ref[...] * (DK ** -0.5 * LOG2E)
    lane = lax.broadcasted_iota(jnp.int32, (t, 2 * DK), 1)
    qa = jnp.where(lane < DK, q, 0.0).astype(BF16)
    qb = jnp.where(lane >= DK, q, 0.0).astype(BF16)
    qs = jnp.concatenate([qa, qb], axis=0)

    m_ref[...] = jnp.full(m_ref.shape, NEG, F32)
    acc_ref[...] = jnp.zeros(acc_ref.shape, F32)

    def scores(blk, bias):
        c0 = pl.multiple_of(blk * t, t)
        s = jnp.dot(qs, kt_ref[:, pl.ds(c0, t)], preferred_element_type=F32)
        if bias is not None:
            s = (s.reshape(2, t, t) + bias[None]).reshape(2 * t, t)
        return s

    def update(blk, s):
        c0 = pl.multiple_of(blk * t, t)
        m_prev = m_ref[...]
        m_next = jnp.maximum(m_prev, jnp.max(s, axis=1, keepdims=True))
        alpha = jnp.exp2(m_prev - m_next)
        p = jnp.exp2(s - jnp.tile(m_next, (1, t // 128)))
        pv = jnp.dot(p.astype(BF16), vx_ref[pl.ds(c0, t), :], preferred_element_type=F32)
        acc_ref[...] = jnp.tile(alpha, (1, 2)) * acc_ref[...] + pv
        m_ref[...] = m_next

    def steps(blocks):
        g, valid = chunk_begin()
        ss = [scores(blk, bias) for blk, bias in blocks]
        css = [chunk_score_block(g, sb) for sb in range(n_sub)]
        for (blk, _), s in zip(blocks, ss):
            update(blk, s)
        chunk_merge(valid, [chunk_softmax_block(g, sb, css[sb]) for sb in range(n_sub)])
        chunk_end(g, valid)

    n_full = jnp.maximum(i - 1, 0)

    def group_body(jj, carry):
        steps([(group * jj + g, None) for g in range(group)])
        return carry
    lax.fori_loop(0, n_full // group, group_body, 0)

    @pl.when(i == 0)
    def _first_tile():
        steps([(0, bdiag_ref[0])])

    for rem in range(group):
        @pl.when(jnp.logical_and(i >= 1, n_full % group == rem))
        def _tail():
            left = [(i - 1 - rem + g, None) for g in range(rem)]
            steps(left + [(i - 1, bsub_ref[0]), (i, bdiag_ref[0])])

    acc = acc_ref[...]
    o_all = acc[:, 0:DV] / acc[:, DV:2 * DV]
    lam = _lambda(lamp_ref, lam_init)
    o_ref[...] = _head_epilogue(o_all[0:t], o_all[t:2 * t], lam, sg_ref[...], ga_ref[...], lam_init)

    @pl.when(last_step)
    def _drain():
        def body(_, carry):
            g, valid = chunk_begin()
            css = [chunk_score_block(g, sb) for sb in range(n_sub)]
            chunk_merge(valid, [chunk_softmax_block(g, sb, css[sb]) for sb in range(n_sub)])
            chunk_end(g, valid)
            return carry
        lax.fori_loop(0, total - count_ref[0], body, 0)


def _attention(z, zs, kc_s, vc_s, cache_k, cache_v, page_table, bsub, bdiag, bpast, bnew, lam_p, subln_g, ds,
               lam_init, t=512, group=2, ppc=8, sub=2048, n_slots=3):
    seq = z.shape[0]
    n_req, n_pages = page_table.shape
    rows = PAGE_SIZE * N_HEADS
    n_pool = cache_k.shape[0]
    nq = N_HEADS * 2 * ds
    assert 2 * ds == 8 and n_pages % ppc == 0 and (ppc * rows) % sub == 0 and sub >= rows
    ck = cache_k.reshape(n_pool, rows, 2 * DK)
    cv = cache_v.reshape(n_pool, rows, DV)
    zs3 = zs.reshape(n_req, ds, D_IN)
    kn = kc_s.reshape(n_req, ds * N_HEADS, 2 * DK)
    vn = vc_s.reshape(n_req, ds * N_HEADS, DV)
    kern = functools.partial(_attn_kernel, t=t, seq=seq, group=group, ds=ds, n_req=n_req, n_pages=n_pages,
                             ppc=ppc, sub=sub, lam_init=lam_init)
    once = pl.Buffered(1)
    grid_spec = pltpu.PrefetchScalarGridSpec(
        num_scalar_prefetch=1,
        grid=(N_HEADS, seq // t),
        in_specs=[
            pl.BlockSpec((t, DV), lambda h, i, pt: (i, QB + h)),
            pl.BlockSpec((seq, DV), lambda h, i, pt: (0, KB + h), pipeline_mode=once),
            pl.BlockSpec((seq, DV), lambda h, i, pt: (0, VB + h), pipeline_mode=once),
            pl.BlockSpec((t, DV), lambda h, i, pt: (i, GAB + h)),
            pl.BlockSpec((1, t, t), lambda h, i, pt: (h, 0, 0), pipeline_mode=once),
            pl.BlockSpec((1, t, t), lambda h, i, pt: (h, 0, 0), pipeline_mode=once),
            pl.BlockSpec((4, DK), lambda h, i, pt: (0, 0)),
            pl.BlockSpec((1, DV), lambda h, i, pt: (0, 0)),
            pl.BlockSpec((n_req, ds, D_ATTN), lambda h, i, pt: (0, 0, QB // N_HEADS)),
            pl.BlockSpec((n_req, ds, D_ATTN), lambda h, i, pt: (0, 0, GAB // N_HEADS)),
            pl.BlockSpec((n_req, ds * N_HEADS, 2 * DK), lambda h, i, pt: (0, 0, 0)),
            pl.BlockSpec((n_req, ds * N_HEADS, DV), lambda h, i, pt: (0, 0, 0)),
            pl.BlockSpec(memory_space=pl.ANY),
            pl.BlockSpec(memory_space=pl.ANY),
            pl.BlockSpec((nq, rows), lambda h, i, pt: (0, 0)),
            pl.BlockSpec((nq, 128), lambda h, i, pt: (0, 0)),
        ],
        out_specs=[
            pl.BlockSpec((t, DV), lambda h, i, pt: (i, h)),
            pl.BlockSpec((n_req, ds, D_ATTN), lambda h, i, pt: (0, 0, 0)),
        ],
        scratch_shapes=[
            pltpu.VMEM((DV, seq), BF16),
            pltpu.VMEM((seq, 2 * DV), BF16),
            pltpu.VMEM((2 * t, 128), F32),
            pltpu.VMEM((2 * t, 2 * DV), F32),
            pltpu.VMEM((n_slots, ppc * rows, 2 * DK), F32),
            pltpu.VMEM((n_slots, ppc * rows, DV), F32),
            pltpu.SemaphoreType.DMA((2, n_slots)),
            pltpu.VMEM((nq, 2 * DK), BF16),
            pltpu.VMEM((nq, 1), F32),
            pltpu.VMEM((nq, 1), F32),
            pltpu.VMEM((nq, DV), F32),
            pltpu.SMEM((1,), jnp.int32),
        ],
    )
    return pl.pallas_call(
        kern,
        grid_spec=grid_spec,
        out_shape=[jax.ShapeDtypeStruct((seq, D_ATTN), F32),
                   jax.ShapeDtypeStruct((n_req, ds, D_ATTN), F32)],
        compiler_params=_params(("arbitrary", "arbitrary"), vmem=ATTN_VMEM_LIMIT),
        name="attention",
    )(page_table, z, z, z, z, bsub, bdiag, lam_p, subln_g, zs3, zs3, kn, vn, ck, cv, bpast, bnew)


def _ln_swish(y, lg, lb):
    mu = jnp.mean(y, axis=-1, keepdims=True)
    yc = y - mu
    var = jnp.mean(yc * yc, axis=-1, keepdims=True)
    return _silu(yc * lax.rsqrt(var + EPS) * lg + lb)


def _pconv_kernel(ca_ref, cb_ref, wdw_ref, bdw_ref, lg_ref, lb_ref, t_ref, nb_ref, ext_ref, sh_ref, wb_ref,
                  y_ref, *, ts, rc):
    i = pl.program_id(0)
    pad = 32
    first = pad - HALO

    @pl.when(i == 0)
    def _():
        ext_ref[0:pad, :] = jnp.zeros((pad, D_CONV), F32)

    ext_ref[pad:pad + ts, :] = ca_ref[...] * jax.nn.sigmoid(cb_ref[...])
    n_sh = sh_ref.shape[1]
    for b in range(1, 8):
        sh_ref[b - 1] = ext_ref[b:b + n_sh, :]
    for j in range(CONV_WIDTH):
        wb_ref[j] = jnp.broadcast_to(wdw_ref[j:j + 1, :], (8, D_CONV))

    for lb in range(D_CONV // 128):
        lanes = slice(lb * 128, (lb + 1) * 128)
        ws = [wb_ref[j, :, lanes] for j in range(CONV_WIDTH)]

        def chunk(r, carry, lanes=lanes, ws=ws):
            r0 = pl.multiple_of(r * rc, rc)
            accs = [jnp.zeros((8, 128), F32) for _ in range(rc // 8)]
            for j in range(CONV_WIDTH):
                a8, b = (first + j) // 8 * 8, (first + j) % 8
                src = ext_ref if b == 0 else sh_ref.at[b - 1]
                for g in range(rc // 8):
                    accs[g] = accs[g] + ws[j] * src[pl.ds(r0 + a8 + 8 * g, 8), lanes]
            y_ref[pl.ds(r0, rc), lanes] = jnp.concatenate(accs, axis=0)
            return carry

        lax.fori_loop(0, ts // rc, chunk, 0)
    y = y_ref[...] + bdw_ref[...]
    t_ref[...] = _ln_swish(y, lg_ref[...], lb_ref[...]).astype(BF16)
    nb_ref[...] = ext_ref[pad + ts - HALO:pad + ts, :]
    ext_ref[0:pad, :] = ext_ref[ts:ts + pad, :]


def _prompt_conv(z, w_dw, b_dw, ln_g, ln_b, ts=512, rc=64):
    seq = z.shape[0]
    vec = pl.BlockSpec((1, D_CONV), lambda i: (0, 0))
    return pl.pallas_call(
        functools.partial(_pconv_kernel, ts=ts, rc=rc),
        grid=(seq // ts,),
        in_specs=[
            pl.BlockSpec((ts, D_CONV), lambda i: (i, CA_BLK)),
            pl.BlockSpec((ts, D_CONV), lambda i: (i, CB_BLK)),
            pl.BlockSpec((CONV_WIDTH, D_CONV), lambda i: (0, 0)),
            vec, vec, vec,
        ],
        out_specs=[
            pl.BlockSpec((ts, D_CONV), lambda i: (i, 0)),
            pl.BlockSpec((HALO, D_CONV), lambda i: (0, 0)),
        ],
        out_shape=[
            jax.ShapeDtypeStruct((seq, D_CONV), BF16),
            jax.ShapeDtypeStruct((HALO, D_CONV), F32),
        ],
        scratch_shapes=[pltpu.VMEM((ts + 32, D_CONV), F32),
                        pltpu.VMEM((7, ts + 24, D_CONV), F32),
                        pltpu.VMEM((CONV_WIDTH, 8, D_CONV), F32),
                        pltpu.VMEM((ts, D_CONV), F32)],
        compiler_params=_params(("arbitrary",)),
        name="prompt_conv",
    )(z, z, w_dw, b_dw, ln_g, ln_b)


def _sconv_kernel(z_ref, st_ref, wdw_ref, bdw_ref, lg_ref, lb_ref, t_ref, nb_ref, ext_ref, *, ds):
    ca = z_ref[0, :, CA_BLK * D_CONV:(CA_BLK + 1) * D_CONV]
    cb = z_ref[0, :, CB_BLK * D_CONV:(CB_BLK + 1) * D_CONV]
    ext_ref[0:HALO, :] = st_ref[0]
    ext_ref[HALO:HALO + ds, :] = ca * jax.nn.sigmoid(cb)
    acc = jnp.zeros((ds, D_CONV), F32)
    for j in range(CONV_WIDTH):
        acc = acc + wdw_ref[j:j + 1, :] * ext_ref[j:j + ds, :]
    y = acc + bdw_ref[...]
    t_ref[0] = _ln_swish(y, lg_ref[...], lb_ref[...])
    nb_ref[0] = ext_ref[ds:ds + HALO, :]


def _sample_conv(zs3, state, w_dw, b_dw, ln_g, ln_b):
    db, ds, _ = zs3.shape
    vec = pl.BlockSpec((1, D_CONV), lambda b: (0, 0))
    return pl.pallas_call(
        functools.partial(_sconv_kernel, ds=ds),
        grid=(db,),
        in_specs=[
            pl.BlockSpec((1, ds, D_IN), lambda b: (b, 0, 0)),
            pl.BlockSpec((1, HALO, D_CONV), lambda b: (b, 0, 0)),
            pl.BlockSpec((CONV_WIDTH, D_CONV), lambda b: (0, 0)),
            vec, vec, vec,
        ],
        out_specs=[
            pl.BlockSpec((1, ds, D_CONV), lambda b: (b, 0, 0)),
            pl.BlockSpec((1, HALO, D_CONV), lambda b: (b, 0, 0)),
        ],
        out_shape=[
            jax.ShapeDtypeStruct((db, ds, D_CONV), F32),
            jax.ShapeDtypeStruct((db, HALO, D_CONV), F32),
        ],
        scratch_shapes=[pltpu.VMEM((HALO + 8, D_CONV), F32)],
        compiler_params=_params(("arbitrary",)),
        name="sample_conv",
    )(zs3, state, w_dw, b_dw, ln_g, ln_b)


def _outproj_kernel(x_ref, ya_ref, t_ref, gc_ref, wpw_ref, bpw_ref, wout_ref, fg_ref, y_ref, wpw_b, wout_b):
    @pl.when(pl.program_id(0) == 0)
    def _():
        wpw_b[...] = wpw_ref[...].astype(BF16)
        wout_b[...] = wout_ref[...].astype(BF16)

    yc = jnp.dot(t_ref[...].astype(BF16), wpw_b[...], preferred_element_type=F32) + bpw_ref[...]
    yc = yc * _silu(gc_ref[...])
    y = jnp.dot(ya_ref[...].astype(BF16), wout_b[0:D_ATTN, :], preferred_element_type=F32)
    y = y + jnp.dot(yc.astype(BF16), wout_b[D_ATTN:D_MODEL, :], preferred_element_type=F32)
    r = x_ref[...] + y
    ms = jnp.mean(r * r, axis=-1, keepdims=True)
    y_ref[...] = r * lax.rsqrt(ms + EPS) * fg_ref[...]


def _outproj(x, y_attn, t, z, w_pw, b_pw, w_out, final_g, tm):
    m = x.shape[0]
    once = pl.Buffered(1)
    return pl.pallas_call(
        _outproj_kernel,
        grid=(m // tm,),
        in_specs=[
            pl.BlockSpec((tm, D_MODEL), lambda i: (i, 0)),
            pl.BlockSpec((tm, D_ATTN), lambda i: (i, 0)),
            pl.BlockSpec((tm, D_CONV), lambda i: (i, 0)),
            pl.BlockSpec((tm, D_CONV), lambda i: (i, GC_BLK)),
            pl.BlockSpec((D_CONV, D_CONV), lambda i: (0, 0), pipeline_mode=once),
            pl.BlockSpec((1, D_CONV), lambda i: (0, 0)),
            pl.BlockSpec((D_MODEL, D_MODEL), lambda i: (0, 0), pipeline_mode=once),
            pl.BlockSpec((1, D_MODEL), lambda i: (0, 0)),
        ],
        out_specs=pl.BlockSpec((tm, D_MODEL), lambda i: (i, 0)),
        out_shape=jax.ShapeDtypeStruct((m, D_MODEL), F32),
        scratch_shapes=[pltpu.VMEM((D_CONV, D_CONV), BF16), pltpu.VMEM((D_MODEL, D_MODEL), BF16)],
        compiler_params=_params(("arbitrary",)),
        name="outproj",
    )(x, y_attn, t, z, w_pw, b_pw, w_out, final_g)


def kernel(x_prompt, x_sample, cache_k, cache_v, state_conv, page_table, norm_g, w_in, lam_p, subln_g,
           w_dw, b_dw, conv_ln_g, conv_ln_b, w_pw, b_pw, w_out, rel_bias, final_g):
    depth = w_in.shape[0]
    assert depth == 1, "single-layer step"
    bsz, seq, _ = x_prompt.shape
    assert bsz == 1
    db, ds, _ = x_sample.shape
    lam_init = 0.8 - 0.6 * math.exp(-0.3 * 0)
    t_attn = 512

    g = norm_g[0][None, :]
    sg = subln_g[0][None, :]
    bdw, lg, lb, bpw = b_dw[0][None, :], conv_ln_g[0][None, :], conv_ln_b[0][None, :], b_pw[0][None, :]
    fg = final_g[None, :]

    xp = x_prompt.reshape(seq, D_MODEL)
    xs = x_sample.reshape(db * ds, D_MODEL)

    zs, kc_s, vc_s, w_in_b = _inproj(xs, g, w_in[0], tm=db * ds, tn=512)
    zp, kc_p, vc_p = _inproj(xp, g, w_in_b, tm=1024, tn=512)
    bsub, bdiag = _prompt_bias(rel_bias, t_attn)
    bpast, bnew = _sample_bias(rel_bias, ds)
    ya_p, ya_s = _attention(zp, zs, kc_s, vc_s, cache_k[0], cache_v[0], page_table, bsub, bdiag, bpast, bnew,
                            lam_p[0], sg, ds, lam_init, t=t_attn)

    t_p, nc_p = _prompt_conv(zp, w_dw[0], bdw, lg, lb)
    y_p = _outproj(xp, ya_p, t_p, zp, w_pw[0], bpw, w_out[0], fg, tm=256)

    zs3 = zs.reshape(db, ds, D_IN)
    t_s, nc_s = _sample_conv(zs3, state_conv[0], w_dw[0], bdw, lg, lb)
    y_s = _outproj(xs, ya_s.reshape(db * ds, D_ATTN), t_s.reshape(db * ds, D_CONV), zs,
                   w_pw[0], bpw, w_out[0], fg, tm=db * ds)

    return (
        y_p.reshape(1, seq, D_MODEL),
        y_s.reshape(db, ds, D_MODEL),
        kc_p.reshape(1, 1, seq, N_HEADS, 2 * DK),
        vc_p.reshape(1, 1, seq, N_HEADS, DV),
        nc_p.reshape(1, 1, HALO, D_CONV),
        kc_s.reshape(1, db, ds, N_HEADS, 2 * DK),
        vc_s.reshape(1, db, ds, N_HEADS, DV),
        nc_s.reshape(1, db, HALO, D_CONV),
    )
```
